```python
import math
import jax
import jax.numpy as jnp
from jax import lax
import numpy as np

D_MODEL = 1024
BATCH = 8
SEQ = 2048
DEPTH = 2
DEC_BATCH = 32
DEC_SEQ = 1
PAST_LEN = 16384
PAGE_SIZE = 128

N_A_LAYERS = DEPTH // 2
N_B_LAYERS = DEPTH - N_A_LAYERS
GDN_HEADS = 8
GDN_DK = D_MODEL // GDN_HEADS
GDN_DV = D_MODEL // GDN_HEADS
GDN_QK_W = GDN_HEADS * GDN_DK
GDN_V_W = GDN_HEADS * GDN_DV
GDN_CONV_CH = 2 * GDN_QK_W + GDN_V_W
GDN_IN_W = GDN_CONV_CH + GDN_V_W + 2 * GDN_HEADS
CONV_W = 4
GDN_CHUNK = 64
MOBA_HEADS = 16
MOBA_HD = D_MODEL // MOBA_HEADS
MOBA_BLOCK = 256
MOBA_TOPK = 3
Q_CHUNK = 8
ALIBI_MAX_EXP = 16
D_FF = 4 * D_MODEL
DN_ALPHA = (2 * DEPTH) ** 0.25
DN_BETA = (8 * DEPTH) ** -0.25
LN_EPS = 1e-5
RMS_EPS = 1e-6

kernel_name = 'yoco_gated_deltanet_moba_step'

F32 = jnp.float32


def _layer_norm(x, g, b):
    xf = x.astype(F32)
    mu = jnp.mean(xf, -1, keepdims=True)
    var = jnp.mean(jnp.square(xf - mu), -1, keepdims=True)
    return (xf - mu) * lax.rsqrt(var + LN_EPS) * g.astype(F32) + b.astype(F32)


def _post_norm(x, sub, g, b):
    return _layer_norm(DN_ALPHA * x.astype(F32) + sub.astype(F32), g, b).astype(x.dtype)


def _sq_relu_mlp(x, w_up, w_down):
    h = jnp.square(jax.nn.relu(jnp.einsum('bld,df->blf', x, w_up)))
    return jnp.einsum('blf,fd->bld', h, w_down)


def _l2norm(x):
    return x * lax.rsqrt(jnp.sum(x * x, -1, keepdims=True) + RMS_EPS)


def _unit_lower_inverse(a):
    c = a.shape[-1]
    t = jnp.eye(c, dtype=a.dtype) + a
    p = a
    for _ in range(int(math.log2(c)) - 1):
        p = jnp.matmul(p, p)
        t = t + jnp.matmul(t, p)
    return t


def _chunk_gated_delta(q, k, v, g, beta, s0):
    B, L, H, _ = q.shape
    c = GDN_CHUNK
    pad = (-L) % c
    nc = (L + pad) // c

    def to_chunks(t):
        t = jnp.pad(t, [(0, 0), (0, pad)] + [(0, 0)] * (t.ndim - 2))
        t = t.reshape((B, nc, c) + t.shape[2:])
        return jnp.moveaxis(t, 3, 1)

    q, k, v, g, beta = (to_chunks(t) for t in (q, k, v, g, beta))
    gc = jnp.cumsum(g, axis=-1)
    pos = jnp.arange(c)
    incl = pos[:, None] >= pos[None, :]
    strict = pos[:, None] > pos[None, :]
    decay = jnp.exp(jnp.where(incl, gc[..., :, None] - gc[..., None, :], -jnp.inf))
    kb = k * beta[..., None]
    vb = v * beta[..., None]
    a = -jnp.where(strict, jnp.einsum('bhncd,bhned->bhnce', kb, k) * decay, 0.0)
    tmat = _unit_lower_inverse(a)
    u = jnp.einsum('bhnce,bhnev->bhncv', tmat, vb)
    w = jnp.einsum('bhnce,bhned->bhncd', tmat, kb * jnp.exp(gc)[..., None])
    qg = q * jnp.exp(gc)[..., None]
    g_last = gc[..., -1]
    k_tail = k * jnp.exp(g_last[..., None] - gc)[..., None]
    attn = jnp.einsum('bhncd,bhned->bhnce', q, k) * decay

    def step(s, xs):
        qg_c, attn_c, u_c, w_c, k_tail_c, g_last_c = xs
        v_new = u_c - jnp.einsum('bhcd,bhdv->bhcv', w_c, s)
        o = jnp.einsum('bhcd,bhdv->bhcv', qg_c, s) + jnp.einsum('bhce,bhev->bhcv', attn_c, v_new)
        s = s * jnp.exp(g_last_c)[..., None, None] + jnp.einsum('bhcd,bhcv->bhdv', k_tail_c, v_new)
        return s, o

    xs = tuple(jnp.moveaxis(t, 2, 0) for t in (qg, attn, u, w, k_tail, g_last))
    s_fin, o = lax.scan(step, s0, xs)
    o = jnp.moveaxis(o, 0, 2).reshape(B, H, nc * c, -1)[:, :, :L]
    return jnp.moveaxis(o, 1, 2), s_fin


def _gated_delta_mixer(x, w_in, w_conv, a_log, dt_bias, w_onorm, w_o, conv_buf, s0):
    B, L, _ = x.shape
    proj = jnp.einsum('bld,de->ble', x, w_in).astype(F32)
    qkv = proj[..., :GDN_CONV_CH]
    z = proj[..., GDN_CONV_CH:GDN_CONV_CH + GDN_V_W]
    a = proj[..., GDN_CONV_CH + GDN_V_W:GDN_CONV_CH + GDN_V_W + GDN_HEADS]
    b = proj[..., GDN_CONV_CH + GDN_V_W + GDN_HEADS:]
    ext = jnp.concatenate([conv_buf.astype(F32), qkv], axis=1)
    wc = w_conv.astype(F32)
    conv = jax.nn.silu(sum(ext[:, i:i + L] * wc[i] for i in range(CONV_W)))
    new_buf = ext[:, L:]
    q = _l2norm(conv[..., :GDN_QK_W].reshape(B, L, GDN_HEADS, GDN_DK)) * GDN_DK ** -0.5
    k = _l2norm(conv[..., GDN_QK_W:2 * GDN_QK_W].reshape(B, L, GDN_HEADS, GDN_DK))
    v = conv[..., 2 * GDN_QK_W:].reshape(B, L, GDN_HEADS, GDN_DV)
    beta = jax.nn.sigmoid(b)
    g = -jnp.exp(a_log.astype(F32)) * jax.nn.softplus(a + dt_bias.astype(F32))
    o, s_new = _chunk_gated_delta(q, k, v, g, beta, s0.astype(F32))
    o = o * lax.rsqrt(jnp.mean(jnp.square(o), -1, keepdims=True) + RMS_EPS) * w_onorm.astype(F32)
    o = o * jax.nn.silu(z.reshape(B, L, GDN_HEADS, GDN_DV))
    out = jnp.einsum('ble,ed->bld', o.reshape(B, L, GDN_V_W), w_o.astype(F32))
    return out, new_buf, s_new


def _shared_kv(x, w_kv):
    B, L, _ = x.shape
    kv = jnp.einsum('bld,de->ble', x, w_kv)
    k = kv[..., :MOBA_HEADS * MOBA_HD].reshape(B, L, MOBA_HEADS, MOBA_HD)
    v = kv[..., MOBA_HEADS * MOBA_HD:].reshape(B, L, MOBA_HEADS, MOBA_HD)
    return k, v


def _alibi_slopes():
    h = jnp.arange(1, MOBA_HEADS + 1, dtype=F32)
    return jnp.exp2(-ALIBI_MAX_EXP * h / MOBA_HEADS)


def _select_blocks(q, k_mean, q_pos, n_sel):
    gate = jnp.einsum('bqhd,bnhd->bqhn', q.astype(F32), k_mean)
    n_full = q_pos // MOBA_BLOCK
    allowed = jnp.arange(k_mean.shape[1])[None, :] < n_full[:, None]
    gate = jnp.where(allowed[None, :, None, :], gate, -jnp.inf)
    _, idx = lax.top_k(gate, n_sel)
    valid = idx < n_full[None, :, None, None]
    return idx, valid


def _moba_core(q, q_pos, k_own, v_own, own_pos, k_sel=None, v_sel=None, sel_pos=None, sel_valid=None):
    slopes = _alibi_slopes()
    qf = q.astype(F32) * MOBA_HD ** -0.5
    dist_own = (q_pos[:, None] - own_pos[None, :]).astype(F32)
    s_own = jnp.einsum('bqhd,bkhd->bqhk', qf, k_own.astype(F32)) - slopes[:, None] * dist_own[:, None, :]
    s_own = jnp.where((dist_own >= 0)[:, None, :], s_own, -jnp.inf)
    if k_sel is None:
        p = jax.nn.softmax(s_own, axis=-1)
        return jnp.einsum('bqhk,bkhd->bqhd', p, v_own.astype(F32))
    dist_sel = (q_pos[None, :, None, None, None] - sel_pos).astype(F32)
    s_sel = jnp.einsum('bqhd,bqhnkd->bqhnk', qf, k_sel.astype(F32)) - slopes[:, None, None] * dist_sel
    s_sel = jnp.where(sel_valid[..., None], s_sel, -jnp.inf)
    b_, nq, nh, n, bl = s_sel.shape
    p = jax.nn.softmax(jnp.concatenate([s_sel.reshape(b_, nq, nh, n * bl), s_own], -1), axis=-1)
    p_sel = p[..., :n * bl].reshape(b_, nq, nh, n, bl)
    p_own = p[..., n * bl:]
    return (jnp.einsum('bqhnk,bqhnkd->bqhd', p_sel, v_sel.astype(F32))
            + jnp.einsum('bqhk,bkhd->bqhd', p_own, v_own.astype(F32)))


def _moba_prompt_attend(q, k, v):
    B, L, H, hd = q.shape
    nb = -(-L // MOBA_BLOCK)
    pad = nb * MOBA_BLOCK - L
    k_pad = jnp.pad(k, ((0, 0), (0, pad), (0, 0), (0, 0)))
    v_pad = jnp.pad(v, ((0, 0), (0, pad), (0, 0), (0, 0)))
    kbt = k_pad.reshape(B, nb, MOBA_BLOCK, H, hd).transpose(0, 3, 1, 2, 4)
    vbt = v_pad.reshape(B, nb, MOBA_BLOCK, H, hd).transpose(0, 3, 1, 2, 4)
    k_mean = jnp.sum(kbt.astype(F32), axis=3).transpose(0, 2, 1, 3) / MOBA_BLOCK
    n_sel = min(MOBA_TOPK, (L - 1) // MOBA_BLOCK)
    qc = math.gcd(Q_CHUNK, L)
    nc = L // qc
    q_chunks = jnp.moveaxis(q.reshape(B, nc, qc, H, hd), 1, 0)
    b_ix = jnp.arange(B)[:, None, None, None]
    h_ix = jnp.arange(H)[None, None, :, None]
    blk = jnp.arange(MOBA_BLOCK, dtype=jnp.int32)

    def one_chunk(args):
        q_c, c = args
        start = c * qc
        q_pos = start + jnp.arange(qc, dtype=jnp.int32)
        own_start = (start // MOBA_BLOCK) * MOBA_BLOCK
        k_own = lax.dynamic_slice_in_dim(k_pad, own_start, MOBA_BLOCK, axis=1)
        v_own = lax.dynamic_slice_in_dim(v_pad, own_start, MOBA_BLOCK, axis=1)
        own_pos = own_start + blk
        if n_sel == 0:
            return _moba_core(q_c, q_pos, k_own, v_own, own_pos)
        idx, valid = _select_blocks(q_c, k_mean, q_pos, n_sel)
        k_sel = kbt[b_ix, h_ix, idx]
        v_sel = vbt[b_ix, h_ix, idx]
        sel_pos = idx[..., None] * MOBA_BLOCK + blk
        return _moba_core(q_c, q_pos, k_own, v_own, own_pos, k_sel, v_sel, sel_pos, valid)

    out = lax.map(one_chunk, (q_chunks, jnp.arange(nc, dtype=jnp.int32)))
    return jnp.moveaxis(out, 0, 1).reshape(B, L, H, hd)


def _moba_layer_prompt(x, k, v, w_q, w_o):
    B, L, _ = x.shape
    q = jnp.einsum('bld,de->ble', x, w_q).reshape(B, L, MOBA_HEADS, MOBA_HD)
    out = _moba_prompt_attend(q, k, v)
    return jnp.einsum('ble,ed->bld', out.reshape(B, L, MOBA_HEADS * MOBA_HD), w_o.astype(F32))


def _moba_layer_sample(x, k_new, v_new, cache_k, cache_v, page_table, w_q, w_o):
    B, Lq, _ = x.shape
    H, hd = MOBA_HEADS, MOBA_HD
    q = jnp.einsum('bld,de->ble', x, w_q).reshape(B, Lq, H, hd)
    n_pages = page_table.shape[1]
    past = n_pages * PAGE_SIZE
    ppb = MOBA_BLOCK // PAGE_SIZE
    nb_full = past // MOBA_BLOCK
    own_start = nb_full * MOBA_BLOCK
    n_tail_pages = n_pages - own_start // PAGE_SIZE
    q_pos = past + jnp.arange(Lq, dtype=jnp.int32)
    if n_tail_pages > 0:
        tail_ids = page_table[:, own_start // PAGE_SIZE:]
        k_tail = cache_k[tail_ids].reshape(B, n_tail_pages * PAGE_SIZE, H, hd)
        v_tail = cache_v[tail_ids].reshape(B, n_tail_pages * PAGE_SIZE, H, hd)
        k_own = jnp.concatenate([k_tail.astype(F32), k_new.astype(F32)], axis=1)
        v_own = jnp.concatenate([v_tail.astype(F32), v_new.astype(F32)], axis=1)
    else:
        k_own, v_own = k_new, v_new
    own_pos = own_start + jnp.arange(k_own.shape[1], dtype=jnp.int32)
    n_sel = min(MOBA_TOPK, nb_full)
    if n_sel > 0:
        k_pages = cache_k[page_table[:, :nb_full * ppb]]
        k_mean = jnp.mean(k_pages.astype(F32).reshape(B, nb_full, MOBA_BLOCK, H, hd), axis=2)
        idx, valid = _select_blocks(q, k_mean, q_pos, n_sel)
        logical = idx[..., None] * ppb + jnp.arange(ppb)
        phys = page_table[jnp.arange(B).reshape(B, 1, 1, 1, 1), logical]
        h_ix = jnp.arange(H).reshape(1, 1, H, 1, 1)
        k_sel = cache_k[phys, :, h_ix].reshape(B, Lq, H, n_sel, MOBA_BLOCK, hd)
        v_sel = cache_v[phys, :, h_ix].reshape(B, Lq, H, n_sel, MOBA_BLOCK, hd)
        sel_pos = idx[..., None] * MOBA_BLOCK + jnp.arange(MOBA_BLOCK, dtype=jnp.int32)
        out = _moba_core(q, q_pos, k_own, v_own, own_pos, k_sel, v_sel, sel_pos, valid)
    else:
        out = _moba_core(q, q_pos, k_own, v_own, own_pos)
    return jnp.einsum('ble,ed->bld', out.reshape(B, Lq, H * hd), w_o.astype(F32))


def setup_inputs(seed: int = 0) -> dict:
    key = jax.random.key(seed)
    ks = jax.random.split(key, 24)

    def nrm(k, shape, scale):
        return jax.random.normal(k, shape, F32) * scale

    n_pages = PAST_LEN // PAGE_SIZE
    n_used = DEC_BATCH * n_pages
    n_pool = -(-5 * n_used // 4)
    x_prompt = nrm(ks[0], (BATCH, SEQ, D_MODEL), 1.0)
    x_sample = nrm(ks[1], (DEC_BATCH, DEC_SEQ, D_MODEL), 1.0)
    cache_k = nrm(ks[2], (n_pool, PAGE_SIZE, MOBA_HEADS, MOBA_HD), 1.0)
    cache_v = nrm(ks[3], (n_pool, PAGE_SIZE, MOBA_HEADS, MOBA_HD), 1.0)
    state_delta = nrm(ks[4], (N_A_LAYERS, DEC_BATCH, GDN_HEADS, GDN_DK, GDN_DV), GDN_DK ** -0.5)
    state_conv = nrm(ks[5], (N_A_LAYERS, DEC_BATCH, CONV_W - 1, GDN_CONV_CH), 1.0)
    page_table = jax.random.permutation(ks[6], n_pool)[:n_used].reshape(DEC_BATCH, n_pages).astype(jnp.int32)

    gdn_w_in = nrm(ks[7], (N_A_LAYERS, D_MODEL, GDN_IN_W), D_MODEL ** -0.5)
    gdn_w_in = gdn_w_in.at[:, :, 2 * GDN_QK_W:GDN_CONV_CH].multiply(DN_BETA)
    gdn_w_conv = nrm(ks[8], (N_A_LAYERS, CONV_W, GDN_CONV_CH), CONV_W ** -0.5)
    gdn_a_log = jnp.log(jax.random.uniform(ks[9], (N_A_LAYERS, GDN_HEADS), F32, 1.0, 16.0))
    dt = jnp.exp(jax.random.uniform(ks[10], (N_A_LAYERS, GDN_HEADS), F32, math.log(1e-3), math.log(1e-1)))
    gdn_dt_bias = dt + jnp.log(-jnp.expm1(-dt))
    gdn_w_onorm = 1.0 + nrm(ks[11], (N_A_LAYERS, GDN_DV), 0.02)
    gdn_w_o = nrm(ks[12], (N_A_LAYERS, GDN_V_W, D_MODEL), GDN_V_W ** -0.5 * DN_BETA)
    w_kv = nrm(ks[13], (D_MODEL, 2 * MOBA_HEADS * MOBA_HD), D_MODEL ** -0.5)
    w_kv = w_kv.at[:, MOBA_HEADS * MOBA_HD:].multiply(DN_BETA)
    moba_w_q = nrm(ks[14], (N_B_LAYERS, D_MODEL, MOBA_HEADS * MOBA_HD), D_MODEL ** -0.5)
    moba_w_o = nrm(ks[15], (N_B_LAYERS, MOBA_HEADS * MOBA_HD, D_MODEL), (MOBA_HEADS * MOBA_HD) ** -0.5 * DN_BETA)
    mlp_w_up = nrm(ks[16], (DEPTH, D_MODEL, D_FF), D_MODEL ** -0.5)
    mlp_w_down = nrm(ks[17], (DEPTH, D_FF, D_MODEL), D_FF ** -0.5 * DN_BETA)
    ln_g = 1.0 + nrm(ks[18], (DEPTH, 2, D_MODEL), 0.02)
    ln_b = nrm(ks[19], (DEPTH, 2, D_MODEL), 0.02)
    return {'x_prompt': x_prompt, 'x_sample': x_sample, 'cache_k': cache_k, 'cache_v': cache_v,
            'state_delta': state_delta, 'state_conv': state_conv, 'page_table': page_table,
            'gdn_w_in': gdn_w_in, 'gdn_w_conv': gdn_w_conv, 'gdn_a_log': gdn_a_log,
            'gdn_dt_bias': gdn_dt_bias, 'gdn_w_onorm': gdn_w_onorm, 'gdn_w_o': gdn_w_o,
            'w_kv': w_kv, 'moba_w_q': moba_w_q, 'moba_w_o': moba_w_o,
            'mlp_w_up': mlp_w_up, 'mlp_w_down': mlp_w_down, 'ln_g': ln_g, 'ln_b': ln_b}


def reference(x_prompt, x_sample, cache_k, cache_v, state_delta, state_conv, page_table,
              gdn_w_in, gdn_w_conv, gdn_a_log, gdn_dt_bias, gdn_w_onorm, gdn_w_o,
              w_kv, moba_w_q, moba_w_o, mlp_w_up, mlp_w_down, ln_g, ln_b):
    xp, xs = x_prompt, x_sample
    bp = xp.shape[0]
    zero_buf = jnp.zeros((bp, CONV_W - 1, GDN_CONV_CH), F32)
    zero_state = jnp.zeros((bp, GDN_HEADS, GDN_DK, GDN_DV), F32)
    conv_p, delta_p, conv_s, delta_s = [], [], [], []
    k_p = v_p = k_s = v_s = None
    for layer in range(DEPTH):
        if layer == N_A_LAYERS:
            k_p, v_p = _shared_kv(xp, w_kv)
            k_s, v_s = _shared_kv(xs, w_kv)
        if layer < N_A_LAYERS:
            i = layer
            wts = (gdn_w_in[i], gdn_w_conv[i], gdn_a_log[i], gdn_dt_bias[i], gdn_w_onorm[i], gdn_w_o[i])
            mp, cb, st = _gated_delta_mixer(xp, *wts, zero_buf, zero_state)
            conv_p.append(cb)
            delta_p.append(st)
            ms, cb, st = _gated_delta_mixer(xs, *wts, state_conv[i], state_delta[i])
            conv_s.append(cb)
            delta_s.append(st)
        else:
            j = layer - N_A_LAYERS
            mp = _moba_layer_prompt(xp, k_p, v_p, moba_w_q[j], moba_w_o[j])
            ms = _moba_layer_sample(xs, k_s, v_s, cache_k, cache_v, page_table, moba_w_q[j], moba_w_o[j])
        xp = _post_norm(xp, mp, ln_g[layer, 0], ln_b[layer, 0])
        xs = _post_norm(xs, ms, ln_g[layer, 0], ln_b[layer, 0])
        xp = _post_norm(xp, _sq_relu_mlp(xp, mlp_w_up[layer], mlp_w_down[layer]), ln_g[layer, 1], ln_b[layer, 1])
        xs = _post_norm(xs, _sq_relu_mlp(xs, mlp_w_up[layer], mlp_w_down[layer]), ln_g[layer, 1], ln_b[layer, 1])
    return (xp, xs, k_p, v_p, jnp.stack(delta_p), jnp.stack(conv_p),
            k_s, v_s, jnp.stack(delta_s), jnp.stack(conv_s))
```

```python
import functools
import math

import jax
import jax.numpy as jnp
from jax import lax
from jax.experimental import pallas as pl
from jax.experimental.pallas import tpu as pltpu

F32 = jnp.float32
BF16 = jnp.bfloat16

GDN_HEADS = 8
GDN_CHUNK = 64
MOBA_HEADS = 16
MOBA_BLOCK = 256
MOBA_TOPK = 3
ALIBI_MAX_EXP = 16
LN_EPS = 1e-5
RMS_EPS = 1e-6

LANES = 128
SUBLANES = 8
VMEM_LIMIT_BYTES = 48 * 1024 * 1024

NEG = -1e30


def _params(*semantics):
    return pltpu.CompilerParams(dimension_semantics=semantics,
                                vmem_limit_bytes=VMEM_LIMIT_BYTES)


def _bdot(a, b):
    return jnp.dot(a.astype(BF16), b.astype(BF16), preferred_element_type=F32)


def _bdot_nt(a, b):
    return lax.dot_general(a.astype(BF16), b.astype(BF16), (((1,), (1,)), ((), ())),
                           preferred_element_type=F32)


def _bdot_tn(a, b):
    return lax.dot_general(a.astype(BF16), b.astype(BF16), (((0,), (0,)), ((), ())),
                           preferred_element_type=F32)


def _split3(a):
    hi = a.astype(BF16)
    r = a - hi.astype(F32)
    mid = r.astype(BF16)
    lo = (r - mid.astype(F32)).astype(BF16)
    return hi, mid, lo


def _layer_norm(h, g, b):
    mu = jnp.mean(h, -1, keepdims=True)
    d = h - mu
    var = jnp.mean(d * d, -1, keepdims=True)
    return d * lax.rsqrt(var + LN_EPS) * g + b


def _silu(x):
    return x * jax.nn.sigmoid(x)


def _softplus(x):
    return jnp.maximum(x, 0.0) + jnp.log(1.0 + jnp.exp(-jnp.abs(x)))


def _chunk_cumsum(g, axis):
    pos = lax.broadcasted_iota(jnp.int32, g.shape, axis) % GDN_CHUNK
    sh = 1
    while sh < GDN_CHUNK:
        g = g + jnp.where(pos >= sh, pltpu.roll(g, sh, axis), 0.0)
        sh *= 2
    return g


def _proj_ln_kernel(alpha, a_ref, x_ref, w_ref, g_ref, b_ref, o_ref):
    sub = jnp.dot(a_ref[...].astype(BF16), w_ref[...], preferred_element_type=F32)
    o_ref[...] = _layer_norm(alpha * x_ref[...] + sub, g_ref[...], b_ref[...])


def _proj_ln(a, x, w, g, b, alpha, tm):
    m, d = x.shape
    e = a.shape[1]
    row = lambda i: (i, 0)
    fixed = lambda i: (0, 0)
    return pl.pallas_call(
        functools.partial(_proj_ln_kernel, alpha),
        grid=(m // tm,),
        in_specs=[pl.BlockSpec((tm, e), row), pl.BlockSpec((tm, d), row),
                  pl.BlockSpec((e, d), fixed), pl.BlockSpec((1, d), fixed),
                  pl.BlockSpec((1, d), fixed)],
        out_specs=pl.BlockSpec((tm, d), row),
        out_shape=jax.ShapeDtypeStruct((m, d), F32),
        compiler_params=_params("parallel"),
        name="proj_ln",
    )(a, x, w, g, b)


def _mlp_kernel(alpha, x_ref, wu_ref, wd_ref, g_ref, b_ref, o_ref, acc_ref):
    f = pl.program_id(1)
    h = jnp.dot(x_ref[...].astype(BF16), wu_ref[...], preferred_element_type=F32)
    h = jnp.square(jnp.maximum(h, 0.0))
    part = jnp.dot(h.astype(BF16), wd_ref[...], preferred_element_type=F32)

    @pl.when(f == 0)
    def _():
        acc_ref[...] = part

    @pl.when(f > 0)
    def _():
        acc_ref[...] += part

    @pl.when(f == pl.num_programs(1) - 1)
    def _():
        o_ref[...] = _layer_norm(alpha * x_ref[...] + acc_ref[...], g_ref[...], b_ref[...])


def _mlp_ln(x, w_up, w_down, g, b, alpha, tm, tf):
    m, d = x.shape
    ff = w_up.shape[1]
    return pl.pallas_call(
        functools.partial(_mlp_kernel, alpha),
        grid=(m // tm, ff // tf),
        in_specs=[pl.BlockSpec((tm, d), lambda i, f: (i, 0)),
                  pl.BlockSpec((d, tf), lambda i, f: (0, f)),
                  pl.BlockSpec((tf, d), lambda i, f: (f, 0)),
                  pl.BlockSpec((1, d), lambda i, f: (0, 0)),
                  pl.BlockSpec((1, d), lambda i, f: (0, 0))],
        out_specs=pl.BlockSpec((tm, d), lambda i, f: (i, 0)),
        out_shape=jax.ShapeDtypeStruct((m, d), F32),
        scratch_shapes=[pltpu.VMEM((tm, d), F32)],
        compiler_params=_params("parallel", "arbitrary"),
        name="mlp_ln",
    )(x, w_up, w_down, g, b)


def _kvq_kernel(with_mean, x_ref, w_ref, k_ref, v_ref, q_ref, *rest):
    e = k_ref.shape[1]
    xb = x_ref[...].astype(BF16)
    k = jnp.dot(xb, w_ref[:, 0:e], preferred_element_type=F32)
    k_ref[...] = k
    v_ref[...] = jnp.dot(xb, w_ref[:, e:2 * e], preferred_element_type=F32)
    q_ref[...] = jnp.dot(xb, w_ref[:, 2 * e:3 * e], preferred_element_type=F32)
    if with_mean:
        rest[0][0] = jnp.sum(k, axis=0, keepdims=True) * (1.0 / k.shape[0])


def _kvq(x, w_kvq, tm, with_mean):
    m, d = x.shape
    e = w_kvq.shape[1] // 3
    row = lambda i: (i, 0)
    out_specs = [pl.BlockSpec((tm, e), row)] * 3
    out_shape = [jax.ShapeDtypeStruct((m, e), F32)] * 3
    if with_mean:
        out_specs.append(pl.BlockSpec((1, 1, e), lambda i: (i, 0, 0)))
        out_shape.append(jax.ShapeDtypeStruct((m // tm, 1, e), F32))
    return pl.pallas_call(
        functools.partial(_kvq_kernel, with_mean),
        grid=(m // tm,),
        in_specs=[pl.BlockSpec((tm, d), row), pl.BlockSpec((d, 3 * e), lambda i: (0, 0))],
        out_specs=out_specs,
        out_shape=out_shape,
        compiler_params=_params("parallel"),
        name="kvq_proj",
    )(x, w_kvq)


def _gate_terms(ab, a_log, dt_bias):
    return -jnp.exp(a_log) * _softplus(ab + dt_bias)


def _gdn_in_kernel(x_ref, w_ref, wab_ref, wabt_ref, wc_ref, alog_r, dtb_r, alog_c, dtb_c,
                   q_ref, k_ref, v_ref, z_ref, gcol_ref, grow_ref, cst_ref,
                   ext_scr, carry_scr):
    t = pl.program_id(1)
    tm = x_ref.shape[1]
    hw = q_ref.shape[1]
    kw = wc_ref.shape[0]
    dk = hw // GDN_HEADS
    xb = x_ref[0].astype(BF16)

    @pl.when(t == 0)
    def _():
        carry_scr[...] = jnp.zeros_like(carry_scr)

    outs = (q_ref, k_ref, v_ref)
    for s in range(3):
        cols = slice(s * hw, (s + 1) * hw)
        pre = jnp.dot(xb, w_ref[:, cols], preferred_element_type=F32)
        ext_scr[0:SUBLANES, :] = carry_scr[s]
        ext_scr[SUBLANES:SUBLANES + tm, :] = pre
        carry_scr[s] = pre[tm - SUBLANES:tm, :]
        conv = pre * wc_ref[kw - 1:kw, cols]
        for sh in range(1, kw):
            conv = conv + ext_scr[pl.ds(SUBLANES - sh, tm), :] * wc_ref[kw - 1 - sh:kw - sh, cols]
        act = _silu(conv)

        @pl.when(t == pl.num_programs(1) - 1)
        def _():
            cst_ref[0, :, cols] = ext_scr[pl.ds(SUBLANES + tm - (kw - 1), kw - 1), :]

        if s == 2:
            outs[s][...] = act
        else:
            scale = dk ** -0.5 if s == 0 else 1.0
            for h in range(GDN_HEADS):
                hs = slice(h * dk, (h + 1) * dk)
                sl = act[:, hs]
                ss = jnp.sum(sl * sl, -1, keepdims=True)
                outs[s][:, hs] = sl * (lax.rsqrt(ss + RMS_EPS) * scale)

    z_ref[...] = jnp.dot(xb, w_ref[:, 3 * hw:4 * hw], preferred_element_type=F32)

    ab = jnp.dot(xb, wab_ref[...], preferred_element_type=F32)
    lane = lax.broadcasted_iota(jnp.int32, ab.shape, 1)
    g = jnp.where(lane < GDN_HEADS, _gate_terms(ab, alog_r[...], dtb_r[...]), 0.0)
    gcol_ref[...] = jnp.where(lane < GDN_HEADS, _chunk_cumsum(g, 0), jax.nn.sigmoid(ab))

    abt = _bdot_nt(wabt_ref[...], xb)
    gct = _chunk_cumsum(_gate_terms(abt, alog_c[...], dtb_c[...]), 1)
    for c in range(tm // GDN_CHUNK):
        grow_ref[0, c] = gct[0:GDN_HEADS, c * GDN_CHUNK:(c + 1) * GDN_CHUNK]


def _gdn_in(x, w_qkvz, w_ab, w_abt, w_conv, alog_r, dtb_r, alog_c, dtb_c, tm):
    b, l, d = x.shape
    hw = w_qkvz.shape[1] // 4
    kw = w_conv.shape[0]
    nt = l // tm
    cpt = tm // GDN_CHUNK
    row = lambda i, t: (i * nt + t, 0)
    fixed = lambda i, t: (0, 0)
    act = jax.ShapeDtypeStruct((b * l, hw), F32)
    return pl.pallas_call(
        _gdn_in_kernel,
        grid=(b, nt),
        in_specs=[pl.BlockSpec((1, tm, d), lambda i, t: (i, t, 0)),
                  pl.BlockSpec(w_qkvz.shape, fixed), pl.BlockSpec(w_ab.shape, fixed),
                  pl.BlockSpec(w_abt.shape, fixed), pl.BlockSpec(w_conv.shape, fixed),
                  pl.BlockSpec(alog_r.shape, fixed), pl.BlockSpec(dtb_r.shape, fixed),
                  pl.BlockSpec(alog_c.shape, fixed), pl.BlockSpec(dtb_c.shape, fixed)],
        out_specs=[pl.BlockSpec((tm, hw), row)] * 4 + [
            pl.BlockSpec((tm, LANES), row),
            pl.BlockSpec((1, cpt, GDN_HEADS, GDN_CHUNK), lambda i, t: (i, t, 0, 0)),
            pl.BlockSpec((1, kw - 1, 3 * hw), lambda i, t: (i, 0, 0))],
        out_shape=[act] * 4 + [
            jax.ShapeDtypeStruct((b * l, LANES), F32),
            jax.ShapeDtypeStruct((b, l // GDN_CHUNK, GDN_HEADS, GDN_CHUNK), F32),
            jax.ShapeDtypeStruct((b, kw - 1, 3 * hw), F32)],
        scratch_shapes=[pltpu.VMEM((SUBLANES + tm, hw), F32),
                        pltpu.VMEM((3, SUBLANES, hw), F32)],
        compiler_params=_params("parallel", "arbitrary"),
        name="gdn_in",
    )(x, w_qkvz, w_ab, w_abt, w_conv, alog_r, dtb_r, alog_c, dtb_c)


def _gdn_chunk_kernel(q_ref, k_ref, v_ref, z_ref, gcol_ref, grow_ref, wn_ref,
                      o_ref, sfin_ref, s_scr):
    c = pl.program_id(1)
    cs = GDN_CHUNK
    dk = q_ref.shape[1] // GDN_HEADS

    @pl.when(c == 0)
    def _():
        s_scr[...] = jnp.zeros_like(s_scr)

    ri = lax.broadcasted_iota(jnp.int32, (cs, cs), 0)
    ci = lax.broadcasted_iota(jnp.int32, (cs, cs), 1)
    incl = ri >= ci
    strict = ri > ci
    eye = (ri == ci).astype(F32)
    gcol = gcol_ref[...]
    grow = grow_ref[0, 0]
    for h in range(GDN_HEADS):
        hs = slice(h * dk, (h + 1) * dk)
        q = q_ref[:, hs]
        k = k_ref[:, hs]
        v = v_ref[:, hs]
        gcc = gcol[:, h:h + 1]
        beta = gcol[:, GDN_HEADS + h:GDN_HEADS + h + 1]
        gcr = grow[h:h + 1, :]
        decay = jnp.exp(jnp.where(incl, gcc - gcr, NEG))
        eg = jnp.exp(gcc)
        glast = gcc[cs - 1:cs, :]
        kb = k * beta
        a = jnp.where(strict, -_bdot_nt(kb, k) * decay, 0.0)
        tm_ = eye + a
        p = a
        for _ in range(int(math.log2(cs)) - 1):
            p = _bdot(p, p)
            tm_ = tm_ + _bdot(tm_, p)
        u = _bdot(tm_, v * beta)
        w = _bdot(tm_, kb * eg)
        attn = _bdot_nt(q, k) * decay
        s = s_scr[h]
        v_new = u - _bdot(w, s)
        o = _bdot(q * eg, s) + _bdot(attn, v_new)
        s_scr[h] = s * jnp.exp(glast) + _bdot_tn(k * jnp.exp(glast - gcc), v_new)
        o = o * lax.rsqrt(jnp.mean(o * o, -1, keepdims=True) + RMS_EPS) * wn_ref[...]
        o_ref[:, hs] = (o * _silu(z_ref[:, hs])).astype(o_ref.dtype)

    @pl.when(c == pl.num_programs(1) - 1)
    def _():
        sfin_ref[0] = s_scr[...]


def _gdn_chunks(q, k, v, z, gcol, grow, w_onorm, b, l):
    hw = q.shape[1]
    dk = hw // GDN_HEADS
    nc = l // GDN_CHUNK
    row = lambda i, c: (i * nc + c, 0)
    return pl.pallas_call(
        _gdn_chunk_kernel,
        grid=(b, nc),
        in_specs=[pl.BlockSpec((GDN_CHUNK, hw), row)] * 4 + [
            pl.BlockSpec((GDN_CHUNK, LANES), row),
            pl.BlockSpec((1, 1, GDN_HEADS, GDN_CHUNK), lambda i, c: (i, c, 0, 0)),
            pl.BlockSpec((1, dk), lambda i, c: (0, 0))],
        out_specs=[pl.BlockSpec((GDN_CHUNK, hw), row),
                   pl.BlockSpec((1, GDN_HEADS, dk, dk), lambda i, c: (i, 0, 0, 0))],
        out_shape=[jax.ShapeDtypeStruct((b * l, hw), BF16),
                   jax.ShapeDtypeStruct((b, GDN_HEADS, dk, dk), F32)],
        scratch_shapes=[pltpu.VMEM((GDN_HEADS, dk, dk), F32)],
        compiler_params=_params("parallel", "arbitrary"),
        name="gdn_chunks",
    )(q, k, v, z, gcol, grow, w_onorm)


def _gdn_in_sample_kernel(x_ref, w_ref, wab_ref, wc_ref, alog_r, dtb_r, sc_ref,
                          q_ref, k_ref, v_ref, z_ref, gate_ref, cst_ref):
    hw = q_ref.shape[1]
    kw = wc_ref.shape[0]
    dk = hw // GDN_HEADS
    xb = x_ref[...].astype(BF16)
    outs = (q_ref, k_ref, v_ref)
    for s in range(3):
        cols = slice(s * hw, (s + 1) * hw)
        pre = jnp.dot(xb, w_ref[:, cols], preferred_element_type=F32)
        conv = pre * wc_ref[kw - 1:kw, cols]
        for j in range(kw - 1):
            conv = conv + sc_ref[j, :, cols] * wc_ref[j:j + 1, cols]
        cst_ref[kw - 2, :, cols] = pre
        act = _silu(conv)
        if s == 2:
            outs[s][...] = act
        else:
            scale = dk ** -0.5 if s == 0 else 1.0
            for h in range(GDN_HEADS):
                hs = slice(h * dk, (h + 1) * dk)
                sl = act[:, hs]
                ss = jnp.sum(sl * sl, -1, keepdims=True)
                outs[s][:, hs] = sl * (lax.rsqrt(ss + RMS_EPS) * scale)
    cst_ref[0:kw - 2] = sc_ref[1:kw - 1]
    z_ref[...] = jnp.dot(xb, w_ref[:, 3 * hw:4 * hw], preferred_element_type=F32)
    ab = jnp.dot(xb, wab_ref[...], preferred_element_type=F32)
    lane = lax.broadcasted_iota(jnp.int32, ab.shape, 1)
    gate_ref[...] = jnp.where(lane < GDN_HEADS, _gate_terms(ab, alog_r[...], dtb_r[...]),
                              jax.nn.sigmoid(ab))


def _gdn_in_sample(x, w_qkvz, w_ab, w_conv, alog_r, dtb_r, conv_state):
    m = x.shape[0]
    hw = w_qkvz.shape[1] // 4
    act = jax.ShapeDtypeStruct((m, hw), F32)
    return pl.pallas_call(
        _gdn_in_sample_kernel,
        out_shape=[act] * 4 + [jax.ShapeDtypeStruct((m, LANES), F32),
                               jax.ShapeDtypeStruct(conv_state.shape, F32)],
        compiler_params=pltpu.CompilerParams(vmem_limit_bytes=VMEM_LIMIT_BYTES),
        name="gdn_in_sample",
    )(x, w_qkvz, w_ab, w_conv, alog_r, dtb_r, conv_state)


def _gdn_step_kernel(q_ref, k_ref, v_ref, z_ref, gate_ref, wn_ref, s_ref, o_ref, snew_ref):
    dk = s_ref.shape[2]
    gate = gate_ref[0]
    for h in range(GDN_HEADS):
        hs = slice(h * dk, (h + 1) * dk)
        q = q_ref[0, :, hs]
        k = k_ref[0, :, hs]
        v = v_ref[0, :, hs]
        eg = jnp.exp(gate[:, h:h + 1])
        beta = gate[:, GDN_HEADS + h:GDN_HEADS + h + 1]
        s = s_ref[0, h]
        kc = jnp.broadcast_to(k, (dk, dk)).T
        qc = jnp.broadcast_to(q, (dk, dk)).T
        ks = jnp.sum(kc * s, axis=0, keepdims=True)
        qs = jnp.sum(qc * s, axis=0, keepdims=True)
        v_new = beta * (v - eg * ks)
        o = eg * qs + jnp.sum(q * k, -1, keepdims=True) * v_new
        snew_ref[0, h] = eg * s + kc * v_new
        o = o * lax.rsqrt(jnp.mean(o * o, -1, keepdims=True) + RMS_EPS) * wn_ref[...]
        o_ref[0, :, hs] = o * _silu(z_ref[0, :, hs])


def _gdn_step(q, k, v, z, gate, w_onorm, state):
    m, hw = q.shape
    dk = hw // GDN_HEADS
    r3 = lambda a: a.reshape(m, 1, a.shape[1])
    tok = pl.BlockSpec((1, 1, hw), lambda i: (i, 0, 0))
    st = pl.BlockSpec((1, GDN_HEADS, dk, dk), lambda i: (i, 0, 0, 0))
    o, s_new = pl.pallas_call(
        _gdn_step_kernel,
        grid=(m,),
        in_specs=[tok] * 4 + [pl.BlockSpec((1, 1, LANES), lambda i: (i, 0, 0)),
                              pl.BlockSpec((1, dk), lambda i: (0, 0)), st],
        out_specs=[tok, st],
        out_shape=[jax.ShapeDtypeStruct((m, 1, hw), F32),
                   jax.ShapeDtypeStruct(state.shape, F32)],
        compiler_params=_params("parallel"),
        name="gdn_step",
    )(r3(q), r3(k), r3(v), r3(z), r3(gate), w_onorm, state)
    return o.reshape(m, hw), s_new


def _moba_prompt_kernel(n_sel, slopes_ref, q_ref, k_ref, v_ref, km_ref, o_ref):
    p = pl.program_id(1)
    i = pl.program_id(2)
    bs = MOBA_BLOCK
    hd = LANES // 2
    nb = km_ref.shape[1]
    q = q_ref[...]
    km = jnp.concatenate([km_ref[0], jnp.zeros((LANES - nb, LANES), F32)], axis=0)
    lane = lax.broadcasted_iota(jnp.int32, (1, LANES), 1)
    jl = lax.broadcasted_iota(jnp.int32, (bs, LANES), 1)
    rel = (lax.broadcasted_iota(jnp.int32, (bs, bs), 0)
           - lax.broadcasted_iota(jnp.int32, (bs, bs), 1)).astype(F32)
    out = jnp.zeros((bs, LANES), F32)
    for hh in range(2):
        mh = (lane // hd) == hh
        qh = jnp.where(mh, q, 0.0)
        q_hi, q_mid, _ = _split3(qh)
        k_hi, k_mid, _ = _split3(km)
        gate = (_bdot_nt(q_hi, k_hi) + _bdot_nt(q_mid, k_hi) + _bdot_nt(q_hi, k_mid))
        cnt = jnp.zeros((bs, LANES), jnp.int32)
        for m in range(nb):
            gm = gate[:, m:m + 1]
            beats = (gm > gate) | ((gm == gate) & (m < jl))
            cnt = cnt + jnp.where(beats, jnp.where(m < i, 1, 0), 0)
        sel = jnp.where((jl < i) & (cnt < n_sel), 1.0, 0.0)

        qb = (qh * hd ** -0.5).astype(BF16)
        slope = slopes_ref[2 * p + hh]

        def scores(j):
            kj = k_ref[pl.ds(pl.multiple_of(j * bs, bs), bs), :]
            dist = rel + ((i - j) * bs).astype(F32)
            return _bdot_nt(qb, kj) - slope * dist

        def values(j):
            return jnp.where(mh, v_ref[pl.ds(pl.multiple_of(j * bs, bs), bs), :], 0.0)

        s = jnp.where(rel >= 0, scores(i), NEG)
        m0 = jnp.max(s, -1, keepdims=True)
        pm = jnp.exp(s - m0)
        l0 = jnp.sum(pm, -1, keepdims=True)
        acc0 = _bdot(pm, values(i))

        def past(j, carry):
            m_prev, l_prev, acc = carry
            keep = jnp.sum(jnp.where(jl == j, sel, 0.0), -1, keepdims=True) > 0.0
            s = jnp.where(keep, scores(j), NEG)
            m_new = jnp.maximum(m_prev, jnp.max(s, -1, keepdims=True))
            alpha = jnp.exp(m_prev - m_new)
            pm = jnp.exp(s - m_new)
            return (m_new, alpha * l_prev + jnp.sum(pm, -1, keepdims=True),
                    alpha * acc + _bdot(pm, values(j)))

        _, l_fin, acc = lax.fori_loop(0, i, past, (m0, l0, acc0))
        out = out + acc / l_fin
    o_ref[...] = out.astype(o_ref.dtype)


def _moba_prompt(q, k, v, k_mean, slopes, b, l):
    e = q.shape[1]
    nb = l // MOBA_BLOCK
    n_sel = min(MOBA_TOPK, (l - 1) // MOBA_BLOCK)
    npair = e // LANES
    return pl.pallas_call(
        functools.partial(_moba_prompt_kernel, n_sel),
        grid=(b, npair, nb),
        in_specs=[pl.BlockSpec(memory_space=pltpu.SMEM),
                  pl.BlockSpec((MOBA_BLOCK, LANES), lambda bi, p, i: (bi * nb + i, p)),
                  pl.BlockSpec((l, LANES), lambda bi, p, i: (bi, p)),
                  pl.BlockSpec((l, LANES), lambda bi, p, i: (bi, p)),
                  pl.BlockSpec((1, nb, LANES), lambda bi, p, i: (bi, 0, p))],
        out_specs=pl.BlockSpec((MOBA_BLOCK, LANES), lambda bi, p, i: (bi * nb + i, p)),
        out_shape=jax.ShapeDtypeStruct((b * l, e), BF16),
        compiler_params=_params("parallel", "parallel", "arbitrary"),
        name="moba_prompt",
    )(slopes, q, k, v, k_mean)


def _moba_select_kernel(n_sel, ppb, bps, pt_ref, q_ref, *refs):
    del pt_ref
    pages = refs[:ppb * bps]
    idx_ref, qcol_scr, gate_scr = refs[ppb * bps:]
    t = pl.program_id(1)
    e, page = qcol_scr.shape
    hd = e // MOBA_HEADS
    nbf = pl.num_programs(1) * bps
    lane = lax.broadcasted_iota(jnp.int32, (MOBA_HEADS, LANES), 1)

    @pl.when(t == 0)
    def _():
        qcol_scr[...] = jnp.broadcast_to(q_ref[0], (page, e)).T
        gate_scr[...] = jnp.zeros_like(gate_scr)

    gate = gate_scr[...]
    for blk in range(bps):
        tot = pages[blk * ppb][0]
        for j in range(1, ppb):
            tot = tot + pages[blk * ppb + j][0]
        per_head = jnp.sum((tot * qcol_scr[...]).reshape(MOBA_HEADS, hd, page), axis=1)
        g = jnp.sum(per_head, -1, keepdims=True) * (1.0 / MOBA_BLOCK)
        gate = jnp.where(lane == t * bps + blk, g, gate)
    gate_scr[...] = gate

    @pl.when(t == pl.num_programs(1) - 1)
    def _():
        lane_f = lane.astype(F32)
        g = jnp.where(lane < nbf, gate, -jnp.inf)
        picks = jnp.zeros((MOBA_HEADS, LANES), F32)
        for r in range(n_sel):
            best = jnp.max(g, -1, keepdims=True)
            idx = jnp.min(jnp.where(g == best, lane_f, float(LANES)), -1, keepdims=True)
            picks = jnp.where(lane == r, idx, picks)
            g = jnp.where(lane_f == idx, -jnp.inf, g)
        idx_ref[0] = picks.astype(jnp.int32)


def _moba_select(q, cache_kt, page_table, n_sel):
    m, e = q.shape
    _, _, page = cache_kt.shape
    ppb = MOBA_BLOCK // page
    nbf = page_table.shape[1] // ppb
    bps = SUBLANES
    assert nbf % bps == 0 and nbf <= LANES and page == LANES

    def page_spec(j):
        return pl.BlockSpec((1, e, page), lambda bi, t, pt: (pt[bi, t * bps * ppb + j], 0, 0))

    return pl.pallas_call(
        functools.partial(_moba_select_kernel, n_sel, ppb, bps),
        grid_spec=pltpu.PrefetchScalarGridSpec(
            num_scalar_prefetch=1,
            grid=(m, nbf // bps),
            in_specs=[pl.BlockSpec((1, 1, e), lambda bi, t, pt: (bi, 0, 0))]
            + [page_spec(j) for j in range(ppb * bps)],
            out_specs=pl.BlockSpec((1, MOBA_HEADS, LANES), lambda bi, t, pt: (bi, 0, 0)),
            scratch_shapes=[pltpu.VMEM((e, page), F32), pltpu.VMEM((MOBA_HEADS, LANES), F32)]),
        out_shape=jax.ShapeDtypeStruct((m, MOBA_HEADS, LANES), jnp.int32),
        compiler_params=_params("parallel", "arbitrary"),
        name="moba_select",
    )(page_table, q.reshape(m, 1, e), *([cache_kt] * (ppb * bps)))


def _moba_sample_kernel(n_sel, ppb, past, pt_ref, sel_ref, slopes_ref, q_ref, kn_ref, vn_ref, *refs):
    del pt_ref
    ntile = 2 * n_sel * ppb
    k_tiles = refs[:ntile]
    v_tiles = refs[ntile:2 * ntile]
    o_ref = refs[2 * ntile]
    bi = pl.program_id(0)
    p = pl.program_id(1)
    hd, page = k_tiles[0].shape[1:]
    pos_in_page = lax.broadcasted_iota(jnp.int32, (SUBLANES, page), 1)
    q = q_ref[0]
    outs = []
    for hh in range(2):
        ls = slice(hh * hd, (hh + 1) * hd)
        h = 2 * p + hh
        slope = slopes_ref[h]
        qh = q[:, ls] * hd ** -0.5
        q8 = jnp.broadcast_to(qh, (SUBLANES, hd))
        s_own = jnp.sum(qh * kn_ref[0][:, ls], -1, keepdims=True)
        scores = []
        for s in range(n_sel):
            blk = sel_ref[bi, h * n_sel + s]
            for j in range(ppb):
                kt = k_tiles[(hh * n_sel + s) * ppb + j][0]
                dist = (past - blk * MOBA_BLOCK - j * page - pos_in_page).astype(F32)
                scores.append(_bdot(q8, kt) - slope * dist)
        mx = s_own
        for sc in scores:
            mx = jnp.maximum(mx, jnp.max(sc, -1, keepdims=True)[0:1])
        p_own = jnp.exp(s_own - mx)
        den = p_own
        acc = p_own * vn_ref[0][:, ls]
        for n, sc in enumerate(scores):
            pm = jnp.exp(sc - mx)
            den = den + jnp.sum(pm, -1, keepdims=True)[0:1]
            acc = acc + _bdot_nt(pm, v_tiles[hh * n_sel * ppb + n][0])[0:1]
        outs.append(acc / den)
    o_ref[0] = jnp.concatenate(outs, axis=-1)


def _moba_sample(q, k_new, v_new, cache_kt, cache_vt, page_table, sel, slopes, n_sel):
    m, e = q.shape
    _, _, page = cache_kt.shape
    hd = e // MOBA_HEADS
    ppb = MOBA_BLOCK // page
    past = page_table.shape[1] * page
    npair = e // LANES
    r3 = lambda a: a.reshape(m, 1, e)
    tok = pl.BlockSpec((1, 1, LANES), lambda bi, p, pt, sl: (bi, 0, p))

    def tile_spec(hh, s, j):
        def index(bi, p, pt, sl):
            h = 2 * p + hh
            return pt[bi, sl[bi, h * n_sel + s] * ppb + j], h, 0
        return pl.BlockSpec((1, hd, page), index)

    tiles = [tile_spec(hh, s, j) for hh in range(2) for s in range(n_sel) for j in range(ppb)]
    out = pl.pallas_call(
        functools.partial(_moba_sample_kernel, n_sel, ppb, past),
        grid_spec=pltpu.PrefetchScalarGridSpec(
            num_scalar_prefetch=2,
            grid=(m, npair),
            in_specs=[pl.BlockSpec(memory_space=pltpu.SMEM), tok, tok, tok] + tiles + tiles,
            out_specs=pl.BlockSpec((1, 1, LANES), lambda bi, p, pt, sl: (bi, 0, p))),
        out_shape=jax.ShapeDtypeStruct((m, 1, e), F32),
        compiler_params=_params("parallel", "parallel"),
        name="moba_sample",
    )(page_table, sel, slopes, r3(q), r3(k_new), r3(v_new),
      *([cache_kt] * len(tiles)), *([cache_vt] * len(tiles)))
    return out.reshape(m, e)


def _row_tile(m, cap):
    t = min(m, cap)
    assert m % t == 0
    return t


def kernel(x_prompt, x_sample, cache_k, cache_v, state_delta, state_conv, page_table,
           gdn_w_in, gdn_w_conv, gdn_a_log, gdn_dt_bias, gdn_w_onorm, gdn_w_o,
           w_kv, moba_w_q, moba_w_o, mlp_w_up, mlp_w_down, ln_g, ln_b):
    bp, lp, d = x_prompt.shape
    bs, ls, _ = x_sample.shape
    depth = mlp_w_up.shape[0]
    n_a = gdn_w_in.shape[0]
    alpha = (2 * depth) ** 0.25
    n_pool, page = cache_k.shape[0], cache_k.shape[1]
    e_kv = cache_k.shape[2] * cache_k.shape[3]
    hw = gdn_w_o.shape[1]
    conv_ch = gdn_w_conv.shape[2]
    kw = gdn_w_conv.shape[1]
    n_pages = page_table.shape[1]
    assert ls == 1, "sample stream handles one new token per sequence"
    assert conv_ch == 3 * hw and kw - 1 <= SUBLANES and kw >= 3
    assert lp % MOBA_BLOCK == 0 and MOBA_BLOCK % GDN_CHUNK == 0 and MOBA_BLOCK % page == 0
    assert (n_pages * page) % MOBA_BLOCK == 0, "past length must end on a MoBA block boundary"
    assert e_kv == MOBA_HEADS * (LANES // 2)
    nbf = n_pages * page // MOBA_BLOCK
    n_sel_s = min(MOBA_TOPK, nbf)
    assert n_sel_s >= 1

    xp = x_prompt.reshape(bp * lp, d)
    xs = x_sample.reshape(bs, d)
    tm_p = _row_tile(bp * lp, 512)
    tm_s = _row_tile(bs, 512)
    row = lambda a: a.reshape(1, -1).astype(F32)
    heads = jnp.arange(1, MOBA_HEADS + 1, dtype=F32)
    slopes = jnp.exp2(-ALIBI_MAX_EXP * heads / MOBA_HEADS)
    cache_kt = cache_k.transpose(0, 2, 3, 1).reshape(n_pool, e_kv, page)
    cache_vt = cache_v.transpose(0, 2, 3, 1).reshape(n_pool, e_kv, page)

    conv_p, delta_p, conv_s, delta_s = [], [], [], []
    k_p = v_p = k_s = v_s = km_p = None
    for layer in range(depth):
        if layer == n_a:
            w_kvq = jnp.concatenate([w_kv, moba_w_q[0]], axis=1).astype(BF16)
        g0, b0 = row(ln_g[layer, 0]), row(ln_b[layer, 0])
        g1, b1 = row(ln_g[layer, 1]), row(ln_b[layer, 1])
        if layer < n_a:
            w_in = gdn_w_in[layer]
            w_qkvz = w_in[:, :4 * hw].astype(BF16)
            w_gate = w_in[:, 4 * hw:]
            w_ab = jnp.pad(w_gate, ((0, 0), (0, LANES - 2 * GDN_HEADS))).astype(BF16)
            w_abt = w_gate.T.astype(BF16)
            pad_r = lambda a: jnp.pad(row(a), ((0, 0), (0, LANES - GDN_HEADS)))
            pad_c = lambda a: jnp.pad(a.astype(F32).reshape(-1, 1), ((0, GDN_HEADS), (0, 0)))
            alog_r, dtb_r = pad_r(gdn_a_log[layer]), pad_r(gdn_dt_bias[layer])
            alog_c, dtb_c = pad_c(gdn_a_log[layer]), pad_c(gdn_dt_bias[layer])
            w_conv = gdn_w_conv[layer].astype(F32)
            w_onorm = row(gdn_w_onorm[layer])
            w_o = gdn_w_o[layer].astype(BF16)

            q, k, v, z, gcol, grow, cst = _gdn_in(
                x_prompt if layer == 0 else xp.reshape(bp, lp, d),
                w_qkvz, w_ab, w_abt, w_conv, alog_r, dtb_r, alog_c, dtb_c, tm=MOBA_BLOCK)
            o, s_fin = _gdn_chunks(q, k, v, z, gcol, grow, w_onorm, bp, lp)
            conv_p.append(cst)
            delta_p.append(s_fin)
            mix_p = (o, w_o)

            q, k, v, z, gate, cst = _gdn_in_sample(
                xs, w_qkvz, w_ab, w_conv, alog_r, dtb_r,
                state_conv[layer].astype(F32).transpose(1, 0, 2))
            o, s_new = _gdn_step(q, k, v, z, gate, w_onorm, state_delta[layer].astype(F32))
            conv_s.append(cst.transpose(1, 0, 2))
            delta_s.append(s_new)
            mix_s = (o, w_o)
        else:
            if layer == n_a:
                k_p, v_p, q_p, km_p = _kvq(xp, w_kvq, MOBA_BLOCK, True)
                k_s, v_s, q_s = _kvq(xs, w_kvq, tm_s, False)
            else:
                w_q = moba_w_q[layer - n_a].astype(BF16)
                zero = jnp.zeros((d, 2 * e_kv), BF16)
                _, _, q_p = _kvq(xp, jnp.concatenate([zero, w_q], axis=1), MOBA_BLOCK, False)
                _, _, q_s = _kvq(xs, jnp.concatenate([zero, w_q], axis=1), tm_s, False)
            w_o = moba_w_o[layer - n_a].astype(BF16)
            o = _moba_prompt(q_p, k_p, v_p, km_p.reshape(bp, lp // MOBA_BLOCK, e_kv), slopes, bp, lp)
            mix_p = (o, w_o)
            sel = _moba_select(q_s, cache_kt, page_table, n_sel_s)
            sel = sel[:, :, :n_sel_s].reshape(bs, MOBA_HEADS * n_sel_s)
            o = _moba_sample(q_s, k_s, v_s, cache_kt, cache_vt, page_table, sel, slopes, n_sel_s)
            mix_s = (o, w_o)
        w_up = mlp_w_up[layer].astype(BF16)
        w_down = mlp_w_down[layer].astype(BF16)
        xp = _proj_ln(mix_p[0], xp, mix_p[1], g0, b0, alpha, tm_p)
        xs = _proj_ln(mix_s[0], xs, mix_s[1], g0, b0, alpha, tm_s)
        xp = _mlp_ln(xp, w_up, w_down, g1, b1, alpha, tm_p, 1024)
        xs = _mlp_ln(xs, w_up, w_down, g1, b1, alpha, tm_s, 1024)

    kv_shape = lambda m, l: (m, l, MOBA_HEADS, e_kv // MOBA_HEADS)
    return (xp.reshape(bp, lp, d), xs.reshape(bs, ls, d),
            k_p.reshape(kv_shape(bp, lp)), v_p.reshape(kv_shape(bp, lp)),
            jnp.stack(delta_p), jnp.stack(conv_p),
            k_s.reshape(kv_shape(bs, ls)), v_s.reshape(kv_shape(bs, ls)),
            jnp.stack(delta_s), jnp.stack(conv_s))
```

```python
import functools
import math

import jax
import jax.numpy as jnp
from jax import lax
from jax.experimental import pallas as pl
from jax.experimental.pallas import tpu as pltpu

F32 = jnp.float32
BF16 = jnp.bfloat16

GDN_HEADS = 8
GDN_CHUNK = 64
MOBA_HEADS = 16
MOBA_BLOCK = 256
MOBA_TOPK = 3
ALIBI_MAX_EXP = 16
LN_EPS = 1e-5
RMS_EPS = 1e-6

LANES = 128
SUBLANES = 8
VMEM_LIMIT_BYTES = 48 * 1024 * 1024

NEG = -1e30
LOG2E = math.log2(math.e)


def _params(*semantics):
    return pltpu.CompilerParams(dimension_semantics=semantics,
                                vmem_limit_bytes=VMEM_LIMIT_BYTES)


def _bdot(a, b):
    return jnp.dot(a.astype(BF16), b.astype(BF16), preferred_element_type=F32)


def _bdot_nt(a, b):
    return lax.dot_general(a.astype(BF16), b.astype(BF16), (((1,), (1,)), ((), ())),
                           preferred_element_type=F32)


def _bdot_tn(a, b):
    return lax.dot_general(a.astype(BF16), b.astype(BF16), (((0,), (0,)), ((), ())),
                           preferred_element_type=F32)


def _split3(a):
    hi = a.astype(BF16)
    r = a - hi.astype(F32)
    mid = r.astype(BF16)
    lo = (r - mid.astype(F32)).astype(BF16)
    return hi, mid, lo


def _layer_norm(h, g, b):
    mu = jnp.mean(h, -1, keepdims=True)
    d = h - mu
    var = jnp.mean(d * d, -1, keepdims=True)
    return d * lax.rsqrt(var + LN_EPS) * g + b


def _silu(x):
    return x * jax.nn.sigmoid(x)


def _softplus(x):
    return jnp.maximum(x, 0.0) + jnp.log(1.0 + jnp.exp(-jnp.abs(x)))


def _chunk_cumsum(g, axis):
    pos = lax.broadcasted_iota(jnp.int32, g.shape, axis) % GDN_CHUNK
    sh = 1
    while sh < GDN_CHUNK:
        g = g + jnp.where(pos >= sh, pltpu.roll(g, sh, axis), 0.0)
        sh *= 2
    return g


def _proj_ln_kernel(alpha, a_ref, x_ref, w_ref, g_ref, b_ref, o_ref):
    sub = jnp.dot(a_ref[...].astype(BF16), w_ref[...], preferred_element_type=F32)
    o_ref[...] = _layer_norm(alpha * x_ref[...] + sub, g_ref[...], b_ref[...])


def _proj_ln(a, x, w, g, b, alpha, tm):
    m, d = x.shape
    e = a.shape[1]
    row = lambda i: (i, 0)
    fixed = lambda i: (0, 0)
    return pl.pallas_call(
        functools.partial(_proj_ln_kernel, alpha),
        grid=(m // tm,),
        in_specs=[pl.BlockSpec((tm, e), row), pl.BlockSpec((tm, d), row),
                  pl.BlockSpec((e, d), fixed), pl.BlockSpec((1, d), fixed),
                  pl.BlockSpec((1, d), fixed)],
        out_specs=pl.BlockSpec((tm, d), row),
        out_shape=jax.ShapeDtypeStruct((m, d), F32),
        compiler_params=_params("parallel"),
        name="proj_ln",
    )(a, x, w, g, b)


def _mlp_kernel(alpha, x_ref, wu_ref, wd_ref, g_ref, b_ref, o_ref, acc_ref):
    f = pl.program_id(1)
    h = jnp.dot(x_ref[...].astype(BF16), wu_ref[...], preferred_element_type=F32)
    h = jnp.square(jnp.maximum(h, 0.0))
    part = jnp.dot(h.astype(BF16), wd_ref[...], preferred_element_type=F32)

    @pl.when(f == 0)
    def _():
        acc_ref[...] = part

    @pl.when(f > 0)
    def _():
        acc_ref[...] += part

    @pl.when(f == pl.num_programs(1) - 1)
    def _():
        o_ref[...] = _layer_norm(alpha * x_ref[...] + acc_ref[...], g_ref[...], b_ref[...])


def _mlp_ln(x, w_up, w_down, g, b, alpha, tm, tf):
    m, d = x.shape
    ff = w_up.shape[1]
    return pl.pallas_call(
        functools.partial(_mlp_kernel, alpha),
        grid=(m // tm, ff // tf),
        in_specs=[pl.BlockSpec((tm, d), lambda i, f: (i, 0)),
                  pl.BlockSpec((d, tf), lambda i, f: (0, f)),
                  pl.BlockSpec((tf, d), lambda i, f: (f, 0)),
                  pl.BlockSpec((1, d), lambda i, f: (0, 0)),
                  pl.BlockSpec((1, d), lambda i, f: (0, 0))],
        out_specs=pl.BlockSpec((tm, d), lambda i, f: (i, 0)),
        out_shape=jax.ShapeDtypeStruct((m, d), F32),
        scratch_shapes=[pltpu.VMEM((tm, d), F32)],
        compiler_params=_params("parallel", "arbitrary"),
        name="mlp_ln",
    )(x, w_up, w_down, g, b)


def _kvq_kernel(x_ref, w_ref, k_ref, v_ref, q_ref):
    e = k_ref.shape[1]
    xb = x_ref[...].astype(BF16)
    k_ref[...] = jnp.dot(xb, w_ref[:, 0:e], preferred_element_type=F32)
    v_ref[...] = jnp.dot(xb, w_ref[:, e:2 * e], preferred_element_type=F32)
    q_ref[...] = jnp.dot(xb, w_ref[:, 2 * e:3 * e], preferred_element_type=F32)


def _kvq(x, w_kvq, tm):
    m, d = x.shape
    e = w_kvq.shape[1] // 3
    row = lambda i: (i, 0)
    return pl.pallas_call(
        _kvq_kernel,
        grid=(m // tm,),
        in_specs=[pl.BlockSpec((tm, d), row), pl.BlockSpec((d, 3 * e), lambda i: (0, 0))],
        out_specs=[pl.BlockSpec((tm, e), row)] * 3,
        out_shape=[jax.ShapeDtypeStruct((m, e), F32)] * 3,
        compiler_params=_params("parallel"),
        name="kvq_proj",
    )(x, w_kvq)


def _kvq_prompt_kernel(x_ref, wk_ref, wvt_ref, wqt_ref, k_ref, kt_ref, vt_ref, qt_ref, km_ref):
    xb = x_ref[...].astype(BF16)
    k = jnp.dot(xb, wk_ref[...], preferred_element_type=F32)
    k_ref[...] = k
    kt_ref[0] = k.T
    vt_ref[0] = _bdot_nt(wvt_ref[...], xb)
    qt_ref[0] = _bdot_nt(wqt_ref[...], xb)
    km_ref[0] = jnp.sum(k, axis=0, keepdims=True) * (1.0 / k.shape[0])


def _kvq_prompt(x, w_k, w_vt, w_qt, b, l):
    m, d = x.shape
    e = w_k.shape[1]
    nb = l // MOBA_BLOCK
    fixed = lambda i, j: (0, 0)
    chan = pl.BlockSpec((1, e, MOBA_BLOCK), lambda i, j: (i, 0, j))
    chan_shape = jax.ShapeDtypeStruct((b, e, l), F32)
    return pl.pallas_call(
        _kvq_prompt_kernel,
        grid=(b, nb),
        in_specs=[pl.BlockSpec((MOBA_BLOCK, d), lambda i, j: (i * nb + j, 0)),
                  pl.BlockSpec((d, e), fixed), pl.BlockSpec((e, d), fixed),
                  pl.BlockSpec((e, d), fixed)],
        out_specs=[pl.BlockSpec((MOBA_BLOCK, e), lambda i, j: (i * nb + j, 0)), chan, chan, chan,
                   pl.BlockSpec((1, 1, e), lambda i, j: (i * nb + j, 0, 0))],
        out_shape=[jax.ShapeDtypeStruct((m, e), F32), chan_shape, chan_shape, chan_shape,
                   jax.ShapeDtypeStruct((m // MOBA_BLOCK, 1, e), F32)],
        compiler_params=_params("parallel", "parallel"),
        name="kvq_prompt",
    )(x, w_k, w_vt, w_qt)


def _gate_terms(ab, a_log, dt_bias):
    return -jnp.exp(a_log) * _softplus(ab + dt_bias)


def _gdn_in_kernel(x_ref, w_ref, wab_ref, wabt_ref, wc_ref, alog_r, dtb_r, alog_c, dtb_c,
                   q_ref, k_ref, v_ref, z_ref, gcol_ref, grow_ref, cst_ref,
                   ext_scr, carry_scr):
    t = pl.program_id(1)
    tm = x_ref.shape[1]
    hw = q_ref.shape[1]
    kw = wc_ref.shape[0]
    dk = hw // GDN_HEADS
    xb = x_ref[0].astype(BF16)

    @pl.when(t == 0)
    def _():
        carry_scr[...] = jnp.zeros_like(carry_scr)

    outs = (q_ref, k_ref, v_ref)
    for s in range(3):
        cols = slice(s * hw, (s + 1) * hw)
        pre = jnp.dot(xb, w_ref[:, cols], preferred_element_type=F32)
        ext_scr[0:SUBLANES, :] = carry_scr[s]
        ext_scr[SUBLANES:SUBLANES + tm, :] = pre
        carry_scr[s] = pre[tm - SUBLANES:tm, :]
        conv = pre * wc_ref[kw - 1:kw, cols]
        for sh in range(1, kw):
            conv = conv + ext_scr[pl.ds(SUBLANES - sh, tm), :] * wc_ref[kw - 1 - sh:kw - sh, cols]
        act = _silu(conv)

        @pl.when(t == pl.num_programs(1) - 1)
        def _():
            cst_ref[0, :, cols] = ext_scr[pl.ds(SUBLANES + tm - (kw - 1), kw - 1), :]

        if s == 2:
            outs[s][...] = act
        else:
            scale = dk ** -0.5 if s == 0 else 1.0
            for h in range(GDN_HEADS):
                hs = slice(h * dk, (h + 1) * dk)
                sl = act[:, hs]
                ss = jnp.sum(sl * sl, -1, keepdims=True)
                outs[s][:, hs] = sl * (lax.rsqrt(ss + RMS_EPS) * scale)

    z_ref[...] = jnp.dot(xb, w_ref[:, 3 * hw:4 * hw], preferred_element_type=F32)

    ab = jnp.dot(xb, wab_ref[...], preferred_element_type=F32)
    lane = lax.broadcasted_iota(jnp.int32, ab.shape, 1)
    g = jnp.where(lane < GDN_HEADS, _gate_terms(ab, alog_r[...], dtb_r[...]), 0.0)
    gcol_ref[...] = jnp.where(lane < GDN_HEADS, _chunk_cumsum(g, 0), jax.nn.sigmoid(ab))

    abt = _bdot_nt(wabt_ref[...], xb)
    gct = _chunk_cumsum(_gate_terms(abt, alog_c[...], dtb_c[...]), 1)
    for c in range(tm // GDN_CHUNK):
        grow_ref[0, c] = gct[0:GDN_HEADS, c * GDN_CHUNK:(c + 1) * GDN_CHUNK]


def _gdn_in(x, w_qkvz, w_ab, w_abt, w_conv, alog_r, dtb_r, alog_c, dtb_c, tm):
    b, l, d = x.shape
    hw = w_qkvz.shape[1] // 4
    kw = w_conv.shape[0]
    nt = l // tm
    cpt = tm // GDN_CHUNK
    row = lambda i, t: (i * nt + t, 0)
    fixed = lambda i, t: (0, 0)
    act = jax.ShapeDtypeStruct((b * l, hw), F32)
    return pl.pallas_call(
        _gdn_in_kernel,
        grid=(b, nt),
        in_specs=[pl.BlockSpec((1, tm, d), lambda i, t: (i, t, 0)),
                  pl.BlockSpec(w_qkvz.shape, fixed), pl.BlockSpec(w_ab.shape, fixed),
                  pl.BlockSpec(w_abt.shape, fixed), pl.BlockSpec(w_conv.shape, fixed),
                  pl.BlockSpec(alog_r.shape, fixed), pl.BlockSpec(dtb_r.shape, fixed),
                  pl.BlockSpec(alog_c.shape, fixed), pl.BlockSpec(dtb_c.shape, fixed)],
        out_specs=[pl.BlockSpec((tm, hw), row)] * 4 + [
            pl.BlockSpec((tm, LANES), row),
            pl.BlockSpec((1, cpt, GDN_HEADS, GDN_CHUNK), lambda i, t: (i, t, 0, 0)),
            pl.BlockSpec((1, kw - 1, 3 * hw), lambda i, t: (i, 0, 0))],
        out_shape=[act] * 4 + [
            jax.ShapeDtypeStruct((b * l, LANES), F32),
            jax.ShapeDtypeStruct((b, l // GDN_CHUNK, GDN_HEADS, GDN_CHUNK), F32),
            jax.ShapeDtypeStruct((b, kw - 1, 3 * hw), F32)],
        scratch_shapes=[pltpu.VMEM((SUBLANES + tm, hw), F32),
                        pltpu.VMEM((3, SUBLANES, hw), F32)],
        compiler_params=_params("parallel", "arbitrary"),
        name="gdn_in",
    )(x, w_qkvz, w_ab, w_abt, w_conv, alog_r, dtb_r, alog_c, dtb_c)


def _gdn_chunk_kernel(q_ref, k_ref, v_ref, z_ref, gcol_ref, grow_ref, wn_ref,
                      o_ref, sfin_ref, s_scr):
    c = pl.program_id(1)
    cs = GDN_CHUNK
    dk = q_ref.shape[1] // GDN_HEADS

    @pl.when(c == 0)
    def _():
        s_scr[...] = jnp.zeros_like(s_scr)

    ri = lax.broadcasted_iota(jnp.int32, (cs, cs), 0)
    ci = lax.broadcasted_iota(jnp.int32, (cs, cs), 1)
    incl = ri >= ci
    strict = ri > ci
    eye = (ri == ci).astype(F32)
    gcol = gcol_ref[...]
    grow = grow_ref[0, 0]
    heads = range(GDN_HEADS)
    hs = [slice(h * dk, (h + 1) * dk) for h in heads]
    q = [q_ref[:, hs[h]].astype(BF16) for h in heads]
    k = [k_ref[:, hs[h]] for h in heads]
    kbf = [k[h].astype(BF16) for h in heads]
    gcc = [gcol[:, h:h + 1] for h in heads]
    beta = [gcol[:, GDN_HEADS + h:GDN_HEADS + h + 1] for h in heads]
    decay = [jnp.exp(jnp.where(incl, gcc[h] - grow[h:h + 1, :], NEG)) for h in heads]
    eg = [jnp.exp(gcc[h]) for h in heads]
    glast = [gcc[h][cs - 1:cs, :] for h in heads]
    kb = [k[h] * beta[h] for h in heads]
    kk = [_bdot_nt(kb[h], kbf[h]) for h in heads]
    qk = [_bdot_nt(q[h], kbf[h]) for h in heads]
    p = [jnp.where(strict, -kk[h] * decay[h], 0.0) for h in heads]
    tinv = [eye + p[h] for h in heads]
    for _ in range(int(math.log2(cs)) - 1):
        pb = [p[h].astype(BF16) for h in heads]
        p = [_bdot(pb[h], pb[h]) for h in heads]
        tinv = [tinv[h] + _bdot(tinv[h], p[h]) for h in heads]
    tb = [tinv[h].astype(BF16) for h in heads]
    u = [_bdot(tb[h], v_ref[:, hs[h]] * beta[h]) for h in heads]
    w = [_bdot(tb[h], kb[h] * eg[h]) for h in heads]
    s = [s_scr[h] for h in heads]
    sb = [s[h].astype(BF16) for h in heads]
    v_new = [(u[h] - _bdot(w[h], sb[h])).astype(BF16) for h in heads]
    o = [_bdot(q_ref[:, hs[h]] * eg[h], sb[h]) + _bdot(qk[h] * decay[h], v_new[h]) for h in heads]
    for h in heads:
        s_scr[h] = (s[h] * jnp.exp(glast[h])
                    + _bdot_tn(k[h] * jnp.exp(glast[h] - gcc[h]), v_new[h]))
    for h in heads:
        on = o[h] * lax.rsqrt(jnp.mean(o[h] * o[h], -1, keepdims=True) + RMS_EPS) * wn_ref[...]
        o_ref[:, hs[h]] = (on * _silu(z_ref[:, hs[h]])).astype(o_ref.dtype)

    @pl.when(c == pl.num_programs(1) - 1)
    def _():
        sfin_ref[0] = s_scr[...]


def _gdn_chunks(q, k, v, z, gcol, grow, w_onorm, b, l):
    hw = q.shape[1]
    dk = hw // GDN_HEADS
    nc = l // GDN_CHUNK
    row = lambda i, c: (i * nc + c, 0)
    return pl.pallas_call(
        _gdn_chunk_kernel,
        grid=(b, nc),
        in_specs=[pl.BlockSpec((GDN_CHUNK, hw), row)] * 4 + [
            pl.BlockSpec((GDN_CHUNK, LANES), row),
            pl.BlockSpec((1, 1, GDN_HEADS, GDN_CHUNK), lambda i, c: (i, c, 0, 0)),
            pl.BlockSpec((1, dk), lambda i, c: (0, 0))],
        out_specs=[pl.BlockSpec((GDN_CHUNK, hw), row),
                   pl.BlockSpec((1, GDN_HEADS, dk, dk), lambda i, c: (i, 0, 0, 0))],
        out_shape=[jax.ShapeDtypeStruct((b * l, hw), BF16),
                   jax.ShapeDtypeStruct((b, GDN_HEADS, dk, dk), F32)],
        scratch_shapes=[pltpu.VMEM((GDN_HEADS, dk, dk), F32)],
        compiler_params=_params("parallel", "arbitrary"),
        name="gdn_chunks",
    )(q, k, v, z, gcol, grow, w_onorm)


def _gdn_in_sample_kernel(x_ref, w_ref, wab_ref, wc_ref, alog_r, dtb_r, sc_ref,
                          q_ref, k_ref, v_ref, z_ref, gate_ref, cst_ref):
    hw = q_ref.shape[1]
    kw = wc_ref.shape[0]
    dk = hw // GDN_HEADS
    xb = x_ref[...].astype(BF16)
    outs = (q_ref, k_ref, v_ref)
    for s in range(3):
        cols = slice(s * hw, (s + 1) * hw)
        pre = jnp.dot(xb, w_ref[:, cols], preferred_element_type=F32)
        conv = pre * wc_ref[kw - 1:kw, cols]
        for j in range(kw - 1):
            conv = conv + sc_ref[j, :, cols] * wc_ref[j:j + 1, cols]
        cst_ref[kw - 2, :, cols] = pre
        act = _silu(conv)
        if s == 2:
            outs[s][...] = act
        else:
            scale = dk ** -0.5 if s == 0 else 1.0
            for h in range(GDN_HEADS):
                hs = slice(h * dk, (h + 1) * dk)
                sl = act[:, hs]
                ss = jnp.sum(sl * sl, -1, keepdims=True)
                outs[s][:, hs] = sl * (lax.rsqrt(ss + RMS_EPS) * scale)
    cst_ref[0:kw - 2] = sc_ref[1:kw - 1]
    z_ref[...] = jnp.dot(xb, w_ref[:, 3 * hw:4 * hw], preferred_element_type=F32)
    ab = jnp.dot(xb, wab_ref[...], preferred_element_type=F32)
    lane = lax.broadcasted_iota(jnp.int32, ab.shape, 1)
    gate_ref[...] = jnp.where(lane < GDN_HEADS, _gate_terms(ab, alog_r[...], dtb_r[...]),
                              jax.nn.sigmoid(ab))


def _gdn_in_sample(x, w_qkvz, w_ab, w_conv, alog_r, dtb_r, conv_state):
    m = x.shape[0]
    hw = w_qkvz.shape[1] // 4
    act = jax.ShapeDtypeStruct((m, hw), F32)
    return pl.pallas_call(
        _gdn_in_sample_kernel,
        out_shape=[act] * 4 + [jax.ShapeDtypeStruct((m, LANES), F32),
                               jax.ShapeDtypeStruct(conv_state.shape, F32)],
        compiler_params=pltpu.CompilerParams(vmem_limit_bytes=VMEM_LIMIT_BYTES),
        name="gdn_in_sample",
    )(x, w_qkvz, w_ab, w_conv, alog_r, dtb_r, conv_state)


def _gdn_step_kernel(q_ref, k_ref, v_ref, z_ref, gate_ref, wn_ref, s_ref, o_ref, snew_ref):
    dk = s_ref.shape[2]
    gate = gate_ref[0]
    for h in range(GDN_HEADS):
        hs = slice(h * dk, (h + 1) * dk)
        q = q_ref[0, :, hs]
        k = k_ref[0, :, hs]
        v = v_ref[0, :, hs]
        eg = jnp.exp(gate[:, h:h + 1])
        beta = gate[:, GDN_HEADS + h:GDN_HEADS + h + 1]
        s = s_ref[0, h]
        kc = jnp.broadcast_to(k, (dk, dk)).T
        qc = jnp.broadcast_to(q, (dk, dk)).T
        ks = jnp.sum(kc * s, axis=0, keepdims=True)
        qs = jnp.sum(qc * s, axis=0, keepdims=True)
        v_new = beta * (v - eg * ks)
        o = eg * qs + jnp.sum(q * k, -1, keepdims=True) * v_new
        snew_ref[0, h] = eg * s + kc * v_new
        o = o * lax.rsqrt(jnp.mean(o * o, -1, keepdims=True) + RMS_EPS) * wn_ref[...]
        o_ref[0, :, hs] = o * _silu(z_ref[0, :, hs])


def _gdn_step(q, k, v, z, gate, w_onorm, state):
    m, hw = q.shape
    dk = hw // GDN_HEADS
    r3 = lambda a: a.reshape(m, 1, a.shape[1])
    tok = pl.BlockSpec((1, 1, hw), lambda i: (i, 0, 0))
    st = pl.BlockSpec((1, GDN_HEADS, dk, dk), lambda i: (i, 0, 0, 0))
    o, s_new = pl.pallas_call(
        _gdn_step_kernel,
        grid=(m,),
        in_specs=[tok] * 4 + [pl.BlockSpec((1, 1, LANES), lambda i: (i, 0, 0)),
                              pl.BlockSpec((1, dk), lambda i: (0, 0)), st],
        out_specs=[tok, st],
        out_shape=[jax.ShapeDtypeStruct((m, 1, hw), F32),
                   jax.ShapeDtypeStruct(state.shape, F32)],
        compiler_params=_params("parallel"),
        name="gdn_step",
    )(r3(q), r3(k), r3(v), r3(z), r3(gate), w_onorm, state)
    return o.reshape(m, hw), s_new


def _moba_prompt_kernel(n_sel, slopes_ref, qt_ref, k_ref, vt_ref, km_ref, o_ref):
    p = pl.program_id(1)
    i = pl.program_id(2)
    bs = MOBA_BLOCK
    hd = LANES // 2
    nb = km_ref.shape[1]
    nbp = -(-nb // SUBLANES) * SUBLANES
    qt = qt_ref[0]
    km = km_ref[0]
    if nbp > nb:
        km = jnp.concatenate([km, jnp.zeros((nbp - nb, LANES), F32)], axis=0)
    k_hi, k_mid, _ = _split3(km)
    chan = lax.broadcasted_iota(jnp.int32, (LANES, bs), 0)
    blk = lax.broadcasted_iota(jnp.int32, (nbp, bs), 0)
    rel = (lax.broadcasted_iota(jnp.int32, (bs, bs), 1)
           - lax.broadcasted_iota(jnp.int32, (bs, bs), 0)).astype(F32)
    wqs, keeps, slope_h = [], [], []
    for hh in range(2):
        qh = jnp.where((chan // hd) == hh, qt, 0.0)
        q_hi, q_mid, _ = _split3(qh)
        gate = _bdot(k_hi, q_hi) + _bdot(k_hi, q_mid) + _bdot(k_mid, q_hi)
        cnt = jnp.zeros((nbp, bs), jnp.int32)
        for m in range(nb):
            gm = gate[m:m + 1, :]
            beats = (gm > gate) | ((gm == gate) & (m < blk))
            cnt = cnt + jnp.where(beats, jnp.where(m < i, 1, 0), 0)
        keeps.append((blk < i) & (cnt < n_sel))
        wqs.append((qh * (hd ** -0.5 * LOG2E)).astype(BF16))
        slope_h.append(slopes_ref[2 * p + hh])

    def attend(n):
        pair = range(2)
        s_all = [_bdot(k_ref[0:n * bs, :], wqs[hh]) for hh in pair]
        us, shifts = [], []
        for hh in pair:
            slope2 = slope_h[hh] * LOG2E
            bias = -slope2 * rel
            u = [s_all[hh][j * bs:(j + 1) * bs, :] + bias for j in range(n - 1)]
            u.append(jnp.where(rel >= 0, s_all[hh][(n - 1) * bs:n * bs, :] + bias, NEG))
            rows = [jnp.where(keeps[hh][j:j + 1, :], -slope2 * float((n - 1 - j) * bs), NEG)
                    for j in range(n - 1)] + [jnp.zeros((1, bs), F32)]
            mx = functools.reduce(
                jnp.maximum, [jnp.max(u[j], 0, keepdims=True) + rows[j] for j in range(n)])
            us.append(u)
            shifts.append([mx - rows[j] for j in range(n)])
        ps = [[jnp.exp2(us[hh][j] - shifts[hh][j]) for j in range(n)] for hh in pair]
        den = [sum(jnp.sum(pm, 0, keepdims=True) for pm in ps[hh]) for hh in pair]
        pcat = [jnp.concatenate([pm.astype(BF16) for pm in ps[hh]], axis=0) for hh in pair]
        acc = [_bdot(vt_ref[0, hh * hd:(hh + 1) * hd, 0:n * bs], pcat[hh]) for hh in pair]
        o_ref[...] = jnp.concatenate([acc[hh] / den[hh] for hh in pair],
                                     axis=0).T.astype(o_ref.dtype)

    for n in range(1, nb + 1):
        pl.when(i == n - 1)(functools.partial(attend, n))


def _moba_prompt(qt, k, vt, k_mean, slopes, b, l):
    e = k.shape[1]
    nb = l // MOBA_BLOCK
    n_sel = min(MOBA_TOPK, (l - 1) // MOBA_BLOCK)
    npair = e // LANES
    return pl.pallas_call(
        functools.partial(_moba_prompt_kernel, n_sel),
        grid=(b, npair, nb),
        in_specs=[pl.BlockSpec(memory_space=pltpu.SMEM),
                  pl.BlockSpec((1, LANES, MOBA_BLOCK), lambda bi, p, i: (bi, p, i)),
                  pl.BlockSpec((l, LANES), lambda bi, p, i: (bi, p)),
                  pl.BlockSpec((1, LANES, l), lambda bi, p, i: (bi, p, 0)),
                  pl.BlockSpec((1, nb, LANES), lambda bi, p, i: (bi, 0, p))],
        out_specs=pl.BlockSpec((MOBA_BLOCK, LANES), lambda bi, p, i: (bi * nb + i, p)),
        out_shape=jax.ShapeDtypeStruct((b * l, e), BF16),
        compiler_params=_params("parallel", "parallel", "arbitrary"),
        name="moba_prompt",
    )(slopes, qt, k, vt, k_mean)


def _moba_select_kernel(n_sel, ppb, bps, pt_ref, q_ref, *refs):
    del pt_ref
    pages = refs[:ppb * bps]
    idx_ref, qcol_scr, gate_scr = refs[ppb * bps:]
    t = pl.program_id(1)
    e, page = qcol_scr.shape
    hd = e // MOBA_HEADS
    nbf = pl.num_programs(1) * bps
    lane = lax.broadcasted_iota(jnp.int32, (MOBA_HEADS, LANES), 1)

    @pl.when(t == 0)
    def _():
        qcol_scr[...] = jnp.broadcast_to(q_ref[0], (page, e)).T
        gate_scr[...] = jnp.zeros_like(gate_scr)

    gate = gate_scr[...]
    for blk in range(bps):
        tot = pages[blk * ppb][0]
        for j in range(1, ppb):
            tot = tot + pages[blk * ppb + j][0]
        per_head = jnp.sum((tot * qcol_scr[...]).reshape(MOBA_HEADS, hd, page), axis=1)
        g = jnp.sum(per_head, -1, keepdims=True) * (1.0 / MOBA_BLOCK)
        gate = jnp.where(lane == t * bps + blk, g, gate)
    gate_scr[...] = gate

    @pl.when(t == pl.num_programs(1) - 1)
    def _():
        lane_f = lane.astype(F32)
        g = jnp.where(lane < nbf, gate, -jnp.inf)
        picks = jnp.zeros((MOBA_HEADS, LANES), F32)
        for r in range(n_sel):
            best = jnp.max(g, -1, keepdims=True)
            idx = jnp.min(jnp.where(g == best, lane_f, float(LANES)), -1, keepdims=True)
            picks = jnp.where(lane == r, idx, picks)
            g = jnp.where(lane_f == idx, -jnp.inf, g)
        idx_ref[0] = picks.astype(jnp.int32)


def _moba_select(q, cache_kt, page_table, n_sel):
    m, e = q.shape
    _, _, page = cache_kt.shape
    ppb = MOBA_BLOCK // page
    nbf = page_table.shape[1] // ppb
    bps = SUBLANES
    assert nbf % bps == 0 and nbf <= LANES and page == LANES

    def page_spec(j):
        return pl.BlockSpec((1, e, page), lambda bi, t, pt: (pt[bi, t * bps * ppb + j], 0, 0))

    return pl.pallas_call(
        functools.partial(_moba_select_kernel, n_sel, ppb, bps),
        grid_spec=pltpu.PrefetchScalarGridSpec(
            num_scalar_prefetch=1,
            grid=(m, nbf // bps),
            in_specs=[pl.BlockSpec((1, 1, e), lambda bi, t, pt: (bi, 0, 0))]
            + [page_spec(j) for j in range(ppb * bps)],
            out_specs=pl.BlockSpec((1, MOBA_HEADS, LANES), lambda bi, t, pt: (bi, 0, 0)),
            scratch_shapes=[pltpu.VMEM((e, page), F32), pltpu.VMEM((MOBA_HEADS, LANES), F32)]),
        out_shape=jax.ShapeDtypeStruct((m, MOBA_HEADS, LANES), jnp.int32),
        compiler_params=_params("parallel", "arbitrary"),
        name="moba_select",
    )(page_table, q.reshape(m, 1, e), *([cache_kt] * (ppb * bps)))


def _moba_sample_kernel(n_sel, ppb, past, pt_ref, sel_ref, slopes_ref, q_ref, kn_ref, vn_ref, *refs):
    del pt_ref
    ntile = 2 * n_sel * ppb
    k_tiles = refs[:ntile]
    v_tiles = refs[ntile:2 * ntile]
    o_ref = refs[2 * ntile]
    bi = pl.program_id(0)
    p = pl.program_id(1)
    hd, page = k_tiles[0].shape[1:]
    pos_in_page = lax.broadcasted_iota(jnp.int32, (SUBLANES, page), 1)
    q = q_ref[0]
    outs = []
    for hh in range(2):
        ls = slice(hh * hd, (hh + 1) * hd)
        h = 2 * p + hh
        slope = slopes_ref[h]
        qh = q[:, ls] * hd ** -0.5
        q8 = jnp.broadcast_to(qh, (SUBLANES, hd))
        s_own = jnp.sum(qh * kn_ref[0][:, ls], -1, keepdims=True)
        scores = []
        for s in range(n_sel):
            blk = sel_ref[bi, h * n_sel + s]
            for j in range(ppb):
                kt = k_tiles[(hh * n_sel + s) * ppb + j][0]
                dist = (past - blk * MOBA_BLOCK - j * page - pos_in_page).astype(F32)
                scores.append(_bdot(q8, kt) - slope * dist)
        mx = s_own
        for sc in scores:
            mx = jnp.maximum(mx, jnp.max(sc, -1, keepdims=True)[0:1])
        p_own = jnp.exp(s_own - mx)
        den = p_own
        acc = p_own * vn_ref[0][:, ls]
        for n, sc in enumerate(scores):
            pm = jnp.exp(sc - mx)
            den = den + jnp.sum(pm, -1, keepdims=True)[0:1]
            acc = acc + _bdot_nt(pm, v_tiles[hh * n_sel * ppb + n][0])[0:1]
        outs.append(acc / den)
    o_ref[0] = jnp.concatenate(outs, axis=-1)


def _moba_sample(q, k_new, v_new, cache_kt, cache_vt, page_table, sel, slopes, n_sel):
    m, e = q.shape
    _, _, page = cache_kt.shape
    hd = e // MOBA_HEADS
    ppb = MOBA_BLOCK // page
    past = page_table.shape[1] * page
    npair = e // LANES
    r3 = lambda a: a.reshape(m, 1, e)
    tok = pl.BlockSpec((1, 1, LANES), lambda bi, p, pt, sl: (bi, 0, p))

    def tile_spec(hh, s, j):
        def index(bi, p, pt, sl):
            h = 2 * p + hh
            return pt[bi, sl[bi, h * n_sel + s] * ppb + j], h, 0
        return pl.BlockSpec((1, hd, page), index)

    tiles = [tile_spec(hh, s, j) for hh in range(2) for s in range(n_sel) for j in range(ppb)]
    out = pl.pallas_call(
        functools.partial(_moba_sample_kernel, n_sel, ppb, past),
        grid_spec=pltpu.PrefetchScalarGridSpec(
            num_scalar_prefetch=2,
            grid=(m, npair),
            in_specs=[pl.BlockSpec(memory_space=pltpu.SMEM), tok, tok, tok] + tiles + tiles,
            out_specs=pl.BlockSpec((1, 1, LANES), lambda bi, p, pt, sl: (bi, 0, p))),
        out_shape=jax.ShapeDtypeStruct((m, 1, e), F32),
        compiler_params=_params("parallel", "parallel"),
        name="moba_sample",
    )(page_table, sel, slopes, r3(q), r3(k_new), r3(v_new),
      *([cache_kt] * len(tiles)), *([cache_vt] * len(tiles)))
    return out.reshape(m, e)


def _row_tile(m, cap):
    t = min(m, cap)
    assert m % t == 0
    return t


def kernel(x_prompt, x_sample, cache_k, cache_v, state_delta, state_conv, page_table,
           gdn_w_in, gdn_w_conv, gdn_a_log, gdn_dt_bias, gdn_w_onorm, gdn_w_o,
           w_kv, moba_w_q, moba_w_o, mlp_w_up, mlp_w_down, ln_g, ln_b):
    bp, lp, d = x_prompt.shape
    bs, ls, _ = x_sample.shape
    depth = mlp_w_up.shape[0]
    n_a = gdn_w_in.shape[0]
    alpha = (2 * depth) ** 0.25
    n_pool, page = cache_k.shape[0], cache_k.shape[1]
    e_kv = cache_k.shape[2] * cache_k.shape[3]
    hw = gdn_w_o.shape[1]
    conv_ch = gdn_w_conv.shape[2]
    kw = gdn_w_conv.shape[1]
    n_pages = page_table.shape[1]
    assert ls == 1, "sample stream handles one new token per sequence"
    assert conv_ch == 3 * hw and kw - 1 <= SUBLANES and kw >= 3
    assert lp % MOBA_BLOCK == 0 and MOBA_BLOCK % GDN_CHUNK == 0 and MOBA_BLOCK % page == 0
    assert (n_pages * page) % MOBA_BLOCK == 0, "past length must end on a MoBA block boundary"
    assert e_kv == MOBA_HEADS * (LANES // 2)
    assert depth - n_a == 1, "one MoBA layer reads the shared K/V"
    nbf = n_pages * page // MOBA_BLOCK
    n_sel_s = min(MOBA_TOPK, nbf)
    assert n_sel_s >= 1

    xp = x_prompt.reshape(bp * lp, d)
    xs = x_sample.reshape(bs, d)
    tm_p = _row_tile(bp * lp, 512)
    tm_s = _row_tile(bs, 512)
    row = lambda a: a.reshape(1, -1).astype(F32)
    heads = jnp.arange(1, MOBA_HEADS + 1, dtype=F32)
    slopes = jnp.exp2(-ALIBI_MAX_EXP * heads / MOBA_HEADS)
    cache_kt = cache_k.transpose(0, 2, 3, 1).reshape(n_pool, e_kv, page)
    cache_vt = cache_v.transpose(0, 2, 3, 1).reshape(n_pool, e_kv, page)

    conv_p, delta_p, conv_s, delta_s = [], [], [], []
    for layer in range(depth):
        g0, b0 = row(ln_g[layer, 0]), row(ln_b[layer, 0])
        g1, b1 = row(ln_g[layer, 1]), row(ln_b[layer, 1])
        if layer < n_a:
            w_in = gdn_w_in[layer]
            w_qkvz = w_in[:, :4 * hw].astype(BF16)
            w_gate = w_in[:, 4 * hw:]
            w_ab = jnp.pad(w_gate, ((0, 0), (0, LANES - 2 * GDN_HEADS))).astype(BF16)
            w_abt = w_gate.T.astype(BF16)
            pad_r = lambda a: jnp.pad(row(a), ((0, 0), (0, LANES - GDN_HEADS)))
            pad_c = lambda a: jnp.pad(a.astype(F32).reshape(-1, 1), ((0, GDN_HEADS), (0, 0)))
            alog_r, dtb_r = pad_r(gdn_a_log[layer]), pad_r(gdn_dt_bias[layer])
            alog_c, dtb_c = pad_c(gdn_a_log[layer]), pad_c(gdn_dt_bias[layer])
            w_conv = gdn_w_conv[layer].astype(F32)
            w_onorm = row(gdn_w_onorm[layer])
            w_o = gdn_w_o[layer].astype(BF16)

            q, k, v, z, gcol, grow, cst = _gdn_in(
                x_prompt if layer == 0 else xp.reshape(bp, lp, d),
                w_qkvz, w_ab, w_abt, w_conv, alog_r, dtb_r, alog_c, dtb_c, tm=MOBA_BLOCK)
            o, s_fin = _gdn_chunks(q, k, v, z, gcol, grow, w_onorm, bp, lp)
            conv_p.append(cst)
            delta_p.append(s_fin)
            mix_p = (o, w_o)

            q, k, v, z, gate, cst = _gdn_in_sample(
                xs, w_qkvz, w_ab, w_conv, alog_r, dtb_r,
                state_conv[layer].astype(F32).transpose(1, 0, 2))
            o, s_new = _gdn_step(q, k, v, z, gate, w_onorm, state_delta[layer].astype(F32))
            conv_s.append(cst.transpose(1, 0, 2))
            delta_s.append(s_new)
            mix_s = (o, w_o)
        else:
            w_q = moba_w_q[layer - n_a]
            k_row, kt_p, vt_p, qt_p, km_p = _kvq_prompt(
                xp, w_kv[:, :e_kv].astype(BF16), w_kv[:, e_kv:].T.astype(BF16),
                w_q.T.astype(BF16), bp, lp)
            k_s, v_s, q_s = _kvq(xs, jnp.concatenate([w_kv, w_q], axis=1).astype(BF16), tm_s)
            w_o = moba_w_o[layer - n_a].astype(BF16)
            o = _moba_prompt(qt_p, k_row, vt_p, km_p.reshape(bp, lp // MOBA_BLOCK, e_kv),
                             slopes, bp, lp)
            mix_p = (o, w_o)
            sel = _moba_select(q_s, cache_kt, page_table, n_sel_s)
            sel = sel[:, :, :n_sel_s].reshape(bs, MOBA_HEADS * n_sel_s)
            o = _moba_sample(q_s, k_s, v_s, cache_kt, cache_vt, page_table, sel, slopes, n_sel_s)
            mix_s = (o, w_o)
        w_up = mlp_w_up[layer].astype(BF16)
        w_down = mlp_w_down[layer].astype(BF16)
        xp = _proj_ln(mix_p[0], xp, mix_p[1], g0, b0, alpha, tm_p)
        xs = _proj_ln(mix_s[0], xs, mix_s[1], g0, b0, alpha, tm_s)
        xp = _mlp_ln(xp, w_up, w_down, g1, b1, alpha, tm_p, 1024)
        xs = _mlp_ln(xs, w_up, w_down, g1, b1, alpha, tm_s, 1024)

    hd = e_kv // MOBA_HEADS
    tokens_major = lambda a: a.reshape(bp, MOBA_HEADS, hd, lp).transpose(0, 3, 1, 2)
    return (xp.reshape(bp, lp, d), xs.reshape(bs, ls, d),
            tokens_major(kt_p), tokens_major(vt_p),
            jnp.stack(delta_p), jnp.stack(conv_p),
            k_s.reshape(bs, ls, MOBA_HEADS, hd), v_s.reshape(bs, ls, MOBA_HEADS, hd),
            jnp.stack(delta_s), jnp.stack(conv_s))
```

```python
import functools
import math

import jax
import jax.numpy as jnp
from jax import lax
from jax.experimental import pallas as pl
from jax.experimental.pallas import tpu as pltpu

F32 = jnp.float32
BF16 = jnp.bfloat16

GDN_HEADS = 8
GDN_CHUNK = 64
MOBA_HEADS = 16
MOBA_BLOCK = 256
MOBA_TOPK = 3
ALIBI_MAX_EXP = 16
LN_EPS = 1e-5
RMS_EPS = 1e-6

LANES = 128
SUBLANES = 8
VMEM_LIMIT_BYTES = 48 * 1024 * 1024

NEG = -1e30
LOG2E = math.log2(math.e)


def _params(*semantics):
    return pltpu.CompilerParams(dimension_semantics=semantics,
                                vmem_limit_bytes=VMEM_LIMIT_BYTES)


def _bdot(a, b):
    return jnp.dot(a.astype(BF16), b.astype(BF16), preferred_element_type=F32)


def _bdot_nt(a, b):
    return lax.dot_general(a.astype(BF16), b.astype(BF16), (((1,), (1,)), ((), ())),
                           preferred_element_type=F32)


def _bdot_tn(a, b):
    return lax.dot_general(a.astype(BF16), b.astype(BF16), (((0,), (0,)), ((), ())),
                           preferred_element_type=F32)


def _split3(a):
    hi = a.astype(BF16)
    r = a - hi.astype(F32)
    mid = r.astype(BF16)
    lo = (r - mid.astype(F32)).astype(BF16)
    return hi, mid, lo


def _layer_norm(h, g, b):
    mu = jnp.mean(h, -1, keepdims=True)
    d = h - mu
    var = jnp.mean(d * d, -1, keepdims=True)
    return d * lax.rsqrt(var + LN_EPS) * g + b


def _silu(x):
    return x * jax.nn.sigmoid(x)


def _softplus(x):
    return jnp.maximum(x, 0.0) + jnp.log(1.0 + jnp.exp(-jnp.abs(x)))


def _chunk_cumsum(g, axis):
    pos = lax.broadcasted_iota(jnp.int32, g.shape, axis) % GDN_CHUNK
    sh = 1
    while sh < GDN_CHUNK:
        g = g + jnp.where(pos >= sh, pltpu.roll(g, sh, axis), 0.0)
        sh *= 2
    return g


def _proj_ln_kernel(alpha, a_ref, x_ref, w_ref, g_ref, b_ref, o_ref):
    sub = jnp.dot(a_ref[...].astype(BF16), w_ref[...], preferred_element_type=F32)
    o_ref[...] = _layer_norm(alpha * x_ref[...] + sub, g_ref[...], b_ref[...])


def _proj_ln(a, x, w, g, b, alpha, tm):
    m, d = x.shape
    e = a.shape[1]
    row = lambda i: (i, 0)
    fixed = lambda i: (0, 0)
    return pl.pallas_call(
        functools.partial(_proj_ln_kernel, alpha),
        grid=(m // tm,),
        in_specs=[pl.BlockSpec((tm, e), row), pl.BlockSpec((tm, d), row),
                  pl.BlockSpec((e, d), fixed), pl.BlockSpec((1, d), fixed),
                  pl.BlockSpec((1, d), fixed)],
        out_specs=pl.BlockSpec((tm, d), row),
        out_shape=jax.ShapeDtypeStruct((m, d), F32),
        compiler_params=_params("parallel"),
        name="proj_ln",
    )(a, x, w, g, b)


def _mlp_kernel(alpha, x_ref, wu_ref, wd_ref, g_ref, b_ref, o_ref, acc_ref):
    f = pl.program_id(1)
    tm = x_ref.shape[0]
    ngrp = 2 if tm % (2 * 2 * SUBLANES) == 0 else 1
    grp = [slice(r * tm // ngrp, (r + 1) * tm // ngrp) for r in range(ngrp)]

    @pl.when(f == 0)
    def _():
        acc_ref[...] = jnp.zeros_like(acc_ref)

    xb = [x_ref[g, :].astype(BF16) for g in grp]
    h = [jnp.dot(xb[r], wu_ref[...], preferred_element_type=F32) for r in range(ngrp)]
    h = [jnp.square(jnp.maximum(h[r], 0.0)).astype(BF16) for r in range(ngrp)]
    part = [jnp.dot(h[r], wd_ref[...], preferred_element_type=F32) for r in range(ngrp)]
    for r in range(ngrp):
        acc_ref[grp[r], :] += part[r]

    @pl.when(f == pl.num_programs(1) - 1)
    def _():
        o_ref[...] = _layer_norm(alpha * x_ref[...] + acc_ref[...], g_ref[...], b_ref[...])


def _mlp_ln(x, w_up, w_down, g, b, alpha, tm, tf):
    m, d = x.shape
    ff = w_up.shape[1]
    return pl.pallas_call(
        functools.partial(_mlp_kernel, alpha),
        grid=(m // tm, ff // tf),
        in_specs=[pl.BlockSpec((tm, d), lambda i, f: (i, 0)),
                  pl.BlockSpec((d, tf), lambda i, f: (0, f)),
                  pl.BlockSpec((tf, d), lambda i, f: (f, 0)),
                  pl.BlockSpec((1, d), lambda i, f: (0, 0)),
                  pl.BlockSpec((1, d), lambda i, f: (0, 0))],
        out_specs=pl.BlockSpec((tm, d), lambda i, f: (i, 0)),
        out_shape=jax.ShapeDtypeStruct((m, d), F32),
        scratch_shapes=[pltpu.VMEM((tm, d), F32)],
        compiler_params=_params("parallel", "arbitrary"),
        name="mlp_ln",
    )(x, w_up, w_down, g, b)


def _kvq_kernel(x_ref, w_ref, k_ref, v_ref, q_ref):
    e = k_ref.shape[1]
    xb = x_ref[...].astype(BF16)
    k_ref[...] = jnp.dot(xb, w_ref[:, 0:e], preferred_element_type=F32)
    v_ref[...] = jnp.dot(xb, w_ref[:, e:2 * e], preferred_element_type=F32)
    q_ref[...] = jnp.dot(xb, w_ref[:, 2 * e:3 * e], preferred_element_type=F32)


def _kvq(x, w_kvq, tm):
    m, d = x.shape
    e = w_kvq.shape[1] // 3
    row = lambda i: (i, 0)
    return pl.pallas_call(
        _kvq_kernel,
        grid=(m // tm,),
        in_specs=[pl.BlockSpec((tm, d), row), pl.BlockSpec((d, 3 * e), lambda i: (0, 0))],
        out_specs=[pl.BlockSpec((tm, e), row)] * 3,
        out_shape=[jax.ShapeDtypeStruct((m, e), F32)] * 3,
        compiler_params=_params("parallel"),
        name="kvq_proj",
    )(x, w_kvq)


def _kvq_prompt_kernel(x_ref, wk_ref, wvt_ref, wqt_ref, k_ref, kt_ref, vt_ref, qt_ref, km_ref):
    xb = x_ref[...].astype(BF16)
    k = jnp.dot(xb, wk_ref[...], preferred_element_type=F32)
    k_ref[...] = k
    kt_ref[0] = k.T
    vt_ref[0] = _bdot_nt(wvt_ref[...], xb)
    qt_ref[0] = _bdot_nt(wqt_ref[...], xb)
    km_ref[0] = jnp.sum(k, axis=0, keepdims=True) * (1.0 / k.shape[0])


def _kvq_prompt(x, w_k, w_vt, w_qt, b, l):
    m, d = x.shape
    e = w_k.shape[1]
    nb = l // MOBA_BLOCK
    fixed = lambda i, j: (0, 0)
    chan = pl.BlockSpec((1, e, MOBA_BLOCK), lambda i, j: (i, 0, j))
    chan_shape = jax.ShapeDtypeStruct((b, e, l), F32)
    return pl.pallas_call(
        _kvq_prompt_kernel,
        grid=(b, nb),
        in_specs=[pl.BlockSpec((MOBA_BLOCK, d), lambda i, j: (i * nb + j, 0)),
                  pl.BlockSpec((d, e), fixed), pl.BlockSpec((e, d), fixed),
                  pl.BlockSpec((e, d), fixed)],
        out_specs=[pl.BlockSpec((MOBA_BLOCK, e), lambda i, j: (i * nb + j, 0)), chan, chan, chan,
                   pl.BlockSpec((1, 1, e), lambda i, j: (i * nb + j, 0, 0))],
        out_shape=[jax.ShapeDtypeStruct((m, e), F32), chan_shape, chan_shape, chan_shape,
                   jax.ShapeDtypeStruct((m // MOBA_BLOCK, 1, e), F32)],
        compiler_params=_params("parallel", "parallel"),
        name="kvq_prompt",
    )(x, w_k, w_vt, w_qt)


def _gate_terms(ab, a_log, dt_bias):
    return -jnp.exp(a_log) * _softplus(ab + dt_bias)


def _gdn_in_kernel(x_ref, w_ref, wab_ref, wabt_ref, wc_ref, alog_r, dtb_r, alog_c, dtb_c,
                   q_ref, k_ref, v_ref, z_ref, gcol_ref, grow_ref, cst_ref,
                   ext_scr):
    t = pl.program_id(1)
    tm = x_ref.shape[1]
    hw = q_ref.shape[1]
    kw = wc_ref.shape[0]
    dk = hw // GDN_HEADS
    xb = x_ref[0].astype(BF16)

    @pl.when(t == 0)
    def _():
        ext_scr[:, 0:SUBLANES, :] = jnp.zeros((3, SUBLANES, hw), F32)

    outs = (q_ref, k_ref, v_ref)
    for s in range(3):
        cols = slice(s * hw, (s + 1) * hw)
        pre = jnp.dot(xb, w_ref[:, cols], preferred_element_type=F32)
        ext_scr[s, SUBLANES:SUBLANES + tm, :] = pre
        conv = pre * wc_ref[kw - 1:kw, cols]
        for sh in range(1, kw):
            conv = conv + (ext_scr[s, pl.ds(SUBLANES - sh, tm), :]
                           * wc_ref[kw - 1 - sh:kw - sh, cols])
        act = _silu(conv)

        @pl.when(t == pl.num_programs(1) - 1)
        def _():
            cst_ref[0, :, cols] = ext_scr[s, pl.ds(SUBLANES + tm - (kw - 1), kw - 1), :]

        ext_scr[s, 0:SUBLANES, :] = pre[tm - SUBLANES:tm, :]

        if s == 2:
            outs[s][...] = act
        else:
            scale = dk ** -0.5 if s == 0 else 1.0
            for h in range(GDN_HEADS):
                hs = slice(h * dk, (h + 1) * dk)
                sl = act[:, hs]
                ss = jnp.sum(sl * sl, -1, keepdims=True)
                outs[s][:, hs] = sl * (lax.rsqrt(ss + RMS_EPS) * scale)

    z_ref[...] = jnp.dot(xb, w_ref[:, 3 * hw:4 * hw], preferred_element_type=F32)

    ab = jnp.dot(xb, wab_ref[...], preferred_element_type=F32)
    lane = lax.broadcasted_iota(jnp.int32, ab.shape, 1)
    g = jnp.where(lane < GDN_HEADS, _gate_terms(ab, alog_r[...], dtb_r[...]), 0.0)
    gcol_ref[...] = jnp.where(lane < GDN_HEADS, _chunk_cumsum(g, 0), jax.nn.sigmoid(ab))

    abt = _bdot_nt(wabt_ref[...], xb)
    gct = _chunk_cumsum(_gate_terms(abt, alog_c[...], dtb_c[...]), 1)
    for c in range(tm // GDN_CHUNK):
        grow_ref[0, c] = gct[0:GDN_HEADS, c * GDN_CHUNK:(c + 1) * GDN_CHUNK]


def _gdn_in(x, w_qkvz, w_ab, w_abt, w_conv, alog_r, dtb_r, alog_c, dtb_c, tm):
    b, l, d = x.shape
    hw = w_qkvz.shape[1] // 4
    kw = w_conv.shape[0]
    nt = l // tm
    cpt = tm // GDN_CHUNK
    row = lambda i, t: (i * nt + t, 0)
    fixed = lambda i, t: (0, 0)
    act = jax.ShapeDtypeStruct((b * l, hw), F32)
    return pl.pallas_call(
        _gdn_in_kernel,
        grid=(b, nt),
        in_specs=[pl.BlockSpec((1, tm, d), lambda i, t: (i, t, 0)),
                  pl.BlockSpec(w_qkvz.shape, fixed), pl.BlockSpec(w_ab.shape, fixed),
                  pl.BlockSpec(w_abt.shape, fixed), pl.BlockSpec(w_conv.shape, fixed),
                  pl.BlockSpec(alog_r.shape, fixed), pl.BlockSpec(dtb_r.shape, fixed),
                  pl.BlockSpec(alog_c.shape, fixed), pl.BlockSpec(dtb_c.shape, fixed)],
        out_specs=[pl.BlockSpec((tm, hw), row)] * 4 + [
            pl.BlockSpec((tm, LANES), row),
            pl.BlockSpec((1, cpt, GDN_HEADS, GDN_CHUNK), lambda i, t: (i, t, 0, 0)),
            pl.BlockSpec((1, kw - 1, 3 * hw), lambda i, t: (i, 0, 0))],
        out_shape=[act] * 4 + [
            jax.ShapeDtypeStruct((b * l, LANES), F32),
            jax.ShapeDtypeStruct((b, l // GDN_CHUNK, GDN_HEADS, GDN_CHUNK), F32),
            jax.ShapeDtypeStruct((b, kw - 1, 3 * hw), F32)],
        scratch_shapes=[pltpu.VMEM((3, SUBLANES + tm, hw), F32)],
        compiler_params=_params("parallel", "arbitrary"),
        name="gdn_in",
    )(x, w_qkvz, w_ab, w_abt, w_conv, alog_r, dtb_r, alog_c, dtb_c)


def _gdn_chunk_kernel(q_ref, k_ref, v_ref, z_ref, gcol_ref, grow_ref, wn_ref,
                      o_ref, sfin_ref, s_scr):
    c = pl.program_id(1)
    cs = GDN_CHUNK
    dk = q_ref.shape[1] // GDN_HEADS

    @pl.when(c == 0)
    def _():
        s_scr[...] = jnp.zeros_like(s_scr)

    ri = lax.broadcasted_iota(jnp.int32, (cs, cs), 0)
    ci = lax.broadcasted_iota(jnp.int32, (cs, cs), 1)
    incl = ri >= ci
    strict = ri > ci
    eye = (ri == ci).astype(F32)
    heads = range(GDN_HEADS)
    cpb = q_ref.shape[0] // cs
    units = [(j, h) for j in range(cpb) for h in heads]
    rows = {j: slice(j * cs, (j + 1) * cs) for j in range(cpb)}
    cols = {h: slice(h * dk, (h + 1) * dk) for h in heads}
    each = lambda f: {unit: f(*unit) for unit in units}
    gcol = {j: gcol_ref[rows[j], :] for j in range(cpb)}
    grow = {j: grow_ref[0, j] for j in range(cpb)}
    q = each(lambda j, h: q_ref[rows[j], cols[h]])
    k = each(lambda j, h: k_ref[rows[j], cols[h]])
    qb = each(lambda j, h: q[j, h].astype(BF16))
    kbf = each(lambda j, h: k[j, h].astype(BF16))
    gcc = each(lambda j, h: gcol[j][:, h:h + 1])
    beta = each(lambda j, h: gcol[j][:, GDN_HEADS + h:GDN_HEADS + h + 1])
    decay = each(lambda j, h: jnp.exp(jnp.where(incl, gcc[j, h] - grow[j][h:h + 1, :], NEG)))
    eg = each(lambda j, h: jnp.exp(gcc[j, h]))
    glast = each(lambda j, h: gcc[j, h][cs - 1:cs, :])
    kb = each(lambda j, h: k[j, h] * beta[j, h])
    kk = each(lambda j, h: _bdot_nt(kb[j, h], kbf[j, h]))
    qk = each(lambda j, h: _bdot_nt(qb[j, h], kbf[j, h]))
    p = each(lambda j, h: jnp.where(strict, -kk[j, h] * decay[j, h], 0.0))
    tinv = each(lambda j, h: eye + p[j, h])
    for _ in range(int(math.log2(cs)) - 1):
        pb = each(lambda j, h: p[j, h].astype(BF16))
        p = each(lambda j, h: _bdot(pb[j, h], pb[j, h]))
        tinv = each(lambda j, h: tinv[j, h] + _bdot(tinv[j, h], p[j, h]))
    tb = each(lambda j, h: tinv[j, h].astype(BF16))
    u = each(lambda j, h: _bdot(tb[j, h], v_ref[rows[j], cols[h]] * beta[j, h]))
    w = each(lambda j, h: _bdot(tb[j, h], kb[j, h] * eg[j, h]).astype(BF16))
    qg = each(lambda j, h: (q[j, h] * eg[j, h]).astype(BF16))
    attn = each(lambda j, h: (qk[j, h] * decay[j, h]).astype(BF16))
    ktail = each(lambda j, h: (k[j, h] * jnp.exp(glast[j, h] - gcc[j, h])).astype(BF16))
    s = [s_scr[h] for h in heads]
    for j in range(cpb):
        sb = [s[h].astype(BF16) for h in heads]
        v_new = [(u[j, h] - _bdot(w[j, h], sb[h])).astype(BF16) for h in heads]
        o = [_bdot(qg[j, h], sb[h]) + _bdot(attn[j, h], v_new[h]) for h in heads]
        s = [s[h] * jnp.exp(glast[j, h]) + _bdot_tn(ktail[j, h], v_new[h]) for h in heads]
        for h in heads:
            on = o[h] * lax.rsqrt(jnp.mean(o[h] * o[h], -1, keepdims=True) + RMS_EPS) * wn_ref[...]
            o_ref[rows[j], cols[h]] = (on * _silu(z_ref[rows[j], cols[h]])).astype(o_ref.dtype)
    for h in heads:
        s_scr[h] = s[h]

    @pl.when(c == pl.num_programs(1) - 1)
    def _():
        sfin_ref[0] = s_scr[...]


def _gdn_chunks(q, k, v, z, gcol, grow, w_onorm, b, l, cpb):
    hw = q.shape[1]
    dk = hw // GDN_HEADS
    nc = l // (cpb * GDN_CHUNK)
    tr = cpb * GDN_CHUNK
    row = lambda i, c: (i * nc + c, 0)
    return pl.pallas_call(
        _gdn_chunk_kernel,
        grid=(b, nc),
        in_specs=[pl.BlockSpec((tr, hw), row)] * 4 + [
            pl.BlockSpec((tr, LANES), row),
            pl.BlockSpec((1, cpb, GDN_HEADS, GDN_CHUNK), lambda i, c: (i, c, 0, 0)),
            pl.BlockSpec((1, dk), lambda i, c: (0, 0))],
        out_specs=[pl.BlockSpec((tr, hw), row),
                   pl.BlockSpec((1, GDN_HEADS, dk, dk), lambda i, c: (i, 0, 0, 0))],
        out_shape=[jax.ShapeDtypeStruct((b * l, hw), BF16),
                   jax.ShapeDtypeStruct((b, GDN_HEADS, dk, dk), F32)],
        scratch_shapes=[pltpu.VMEM((GDN_HEADS, dk, dk), F32)],
        compiler_params=_params("parallel", "arbitrary"),
        name="gdn_chunks",
    )(q, k, v, z, gcol, grow, w_onorm)


def _gdn_in_sample_kernel(x_ref, w_ref, wab_ref, wc_ref, alog_r, dtb_r, sc_ref,
                          q_ref, k_ref, v_ref, z_ref, gate_ref, cst_ref):
    hw = q_ref.shape[1]
    kw = wc_ref.shape[0]
    dk = hw // GDN_HEADS
    xb = x_ref[...].astype(BF16)
    outs = (q_ref, k_ref, v_ref)
    for s in range(3):
        cols = slice(s * hw, (s + 1) * hw)
        pre = jnp.dot(xb, w_ref[:, cols], preferred_element_type=F32)
        conv = pre * wc_ref[kw - 1:kw, cols]
        for j in range(kw - 1):
            conv = conv + sc_ref[j, :, cols] * wc_ref[j:j + 1, cols]
        cst_ref[kw - 2, :, cols] = pre
        act = _silu(conv)
        if s == 2:
            outs[s][...] = act
        else:
            scale = dk ** -0.5 if s == 0 else 1.0
            for h in range(GDN_HEADS):
                hs = slice(h * dk, (h + 1) * dk)
                sl = act[:, hs]
                ss = jnp.sum(sl * sl, -1, keepdims=True)
                outs[s][:, hs] = sl * (lax.rsqrt(ss + RMS_EPS) * scale)
    cst_ref[0:kw - 2] = sc_ref[1:kw - 1]
    z_ref[...] = jnp.dot(xb, w_ref[:, 3 * hw:4 * hw], preferred_element_type=F32)
    ab = jnp.dot(xb, wab_ref[...], preferred_element_type=F32)
    lane = lax.broadcasted_iota(jnp.int32, ab.shape, 1)
    gate_ref[...] = jnp.where(lane < GDN_HEADS, _gate_terms(ab, alog_r[...], dtb_r[...]),
                              jax.nn.sigmoid(ab))


def _gdn_in_sample(x, w_qkvz, w_ab, w_conv, alog_r, dtb_r, conv_state):
    m = x.shape[0]
    hw = w_qkvz.shape[1] // 4
    act = jax.ShapeDtypeStruct((m, hw), F32)
    return pl.pallas_call(
        _gdn_in_sample_kernel,
        out_shape=[act] * 4 + [jax.ShapeDtypeStruct((m, LANES), F32),
                               jax.ShapeDtypeStruct(conv_state.shape, F32)],
        compiler_params=pltpu.CompilerParams(vmem_limit_bytes=VMEM_LIMIT_BYTES),
        name="gdn_in_sample",
    )(x, w_qkvz, w_ab, w_conv, alog_r, dtb_r, conv_state)


def _gdn_step_kernel(q_ref, k_ref, v_ref, z_ref, gate_ref, wn_ref, s_ref, o_ref, snew_ref):
    dk = s_ref.shape[2]
    gate = gate_ref[0]
    for h in range(GDN_HEADS):
        hs = slice(h * dk, (h + 1) * dk)
        q = q_ref[0, :, hs]
        k = k_ref[0, :, hs]
        v = v_ref[0, :, hs]
        eg = jnp.exp(gate[:, h:h + 1])
        beta = gate[:, GDN_HEADS + h:GDN_HEADS + h + 1]
        s = s_ref[0, h]
        kc = jnp.broadcast_to(k, (dk, dk)).T
        qc = jnp.broadcast_to(q, (dk, dk)).T
        ks = jnp.sum(kc * s, axis=0, keepdims=True)
        qs = jnp.sum(qc * s, axis=0, keepdims=True)
        v_new = beta * (v - eg * ks)
        o = eg * qs + jnp.sum(q * k, -1, keepdims=True) * v_new
        snew_ref[0, h] = eg * s + kc * v_new
        o = o * lax.rsqrt(jnp.mean(o * o, -1, keepdims=True) + RMS_EPS) * wn_ref[...]
        o_ref[0, :, hs] = o * _silu(z_ref[0, :, hs])


def _gdn_step(q, k, v, z, gate, w_onorm, state):
    m, hw = q.shape
    dk = hw // GDN_HEADS
    r3 = lambda a: a.reshape(m, 1, a.shape[1])
    tok = pl.BlockSpec((1, 1, hw), lambda i: (i, 0, 0))
    st = pl.BlockSpec((1, GDN_HEADS, dk, dk), lambda i: (i, 0, 0, 0))
    o, s_new = pl.pallas_call(
        _gdn_step_kernel,
        grid=(m,),
        in_specs=[tok] * 4 + [pl.BlockSpec((1, 1, LANES), lambda i: (i, 0, 0)),
                              pl.BlockSpec((1, dk), lambda i: (0, 0)), st],
        out_specs=[tok, st],
        out_shape=[jax.ShapeDtypeStruct((m, 1, hw), F32),
                   jax.ShapeDtypeStruct(state.shape, F32)],
        compiler_params=_params("parallel"),
        name="gdn_step",
    )(r3(q), r3(k), r3(v), r3(z), r3(gate), w_onorm, state)
    return o.reshape(m, hw), s_new


def _moba_prompt_kernel(n_sel, slopes_ref, qt_ref, k_ref, vt_ref, km_ref, o_ref):
    p = pl.program_id(1)
    i = pl.program_id(2)
    bs = MOBA_BLOCK
    hd = LANES // 2
    nb = km_ref.shape[1]
    nbp = -(-nb // SUBLANES) * SUBLANES
    qt = qt_ref[0]
    km = km_ref[0]
    if nbp > nb:
        km = jnp.concatenate([km, jnp.zeros((nbp - nb, LANES), F32)], axis=0)
    k_hi, k_mid, _ = _split3(km)
    chan = lax.broadcasted_iota(jnp.int32, (LANES, bs), 0)
    blk = lax.broadcasted_iota(jnp.int32, (nbp, bs), 0)
    rel = (lax.broadcasted_iota(jnp.int32, (bs, bs), 1)
           - lax.broadcasted_iota(jnp.int32, (bs, bs), 0))
    q_idx = lax.broadcasted_iota(jnp.int32, (1, bs), 1).astype(F32)
    aug_row = lax.broadcasted_iota(jnp.int32, (2 * SUBLANES, bs), 0)
    aug_lane = lax.broadcasted_iota(jnp.int32, (bs, LANES), 1)
    k_idx = lax.broadcasted_iota(jnp.int32, (bs, LANES), 0).astype(F32)
    k_aug = jnp.where(aug_lane < 3, k_idx, jnp.where(aug_lane < 6, 1.0, 0.0)).astype(BF16)
    wqs, keeps, slope_h = [], [], []
    for hh in range(2):
        qh = jnp.where((chan // hd) == hh, qt, 0.0)
        q_hi, q_mid, _ = _split3(qh)
        gate = _bdot(k_hi, q_hi) + _bdot(k_hi, q_mid) + _bdot(k_mid, q_hi)
        cnt = jnp.zeros((nbp, bs), jnp.int32)
        for m in range(nb):
            gm = gate[m:m + 1, :]
            beats = (gm > gate) | ((gm == gate) & (m < blk))
            cnt = cnt + jnp.where(beats, jnp.where(m < i, 1, 0), 0)
        keeps.append((blk < i) & (cnt < n_sel))
        slope2 = slopes_ref[2 * p + hh] * LOG2E
        slope_h.append(slope2)
        s_parts = _split3(jnp.full((1, bs), slope2, F32))
        t_parts = _split3(-slope2 * q_idx)
        aug = jnp.zeros((2 * SUBLANES, bs), F32)
        for r, part in enumerate(s_parts + t_parts):
            aug = jnp.where(aug_row == r, part.astype(F32), aug)
        wqs.append(jnp.concatenate(
            [(qh * (hd ** -0.5 * LOG2E)).astype(BF16), aug.astype(BF16),
             jnp.zeros((LANES - 2 * SUBLANES, bs), BF16)], axis=0))

    def attend(n):
        pair = range(2)
        kcat = jnp.concatenate([k_ref[0:n * bs, :].astype(BF16),
                                jnp.concatenate([k_aug] * n, axis=0)], axis=1)
        s_all = [jnp.dot(kcat, wqs[hh], preferred_element_type=F32) for hh in pair]
        us, shifts = [], []
        for hh in pair:
            u = [s_all[hh][j * bs:(j + 1) * bs, :] for j in range(n - 1)]
            u.append(jnp.where(rel >= 0, s_all[hh][(n - 1) * bs:n * bs, :], NEG))
            rows = [jnp.where(keeps[hh][j:j + 1, :], -slope_h[hh] * float((n - 1 - j) * bs), NEG)
                    for j in range(n - 1)] + [jnp.zeros((1, bs), F32)]
            mx = functools.reduce(
                jnp.maximum, [jnp.max(u[j], 0, keepdims=True) + rows[j] for j in range(n)])
            us.append(u)
            shifts.append([mx - rows[j] for j in range(n)])
        ps = [[jnp.exp2(us[hh][j] - shifts[hh][j]) for j in range(n)] for hh in pair]
        den = [sum(jnp.sum(pm, 0, keepdims=True) for pm in ps[hh]) for hh in pair]
        pcat = [jnp.concatenate([pm.astype(BF16) for pm in ps[hh]], axis=0) for hh in pair]
        acc = [_bdot(vt_ref[0, hh * hd:(hh + 1) * hd, 0:n * bs], pcat[hh]) for hh in pair]
        o_ref[...] = jnp.concatenate([acc[hh] / den[hh] for hh in pair],
                                     axis=0).T.astype(o_ref.dtype)

    for n in range(1, nb + 1):
        pl.when(i == n - 1)(functools.partial(attend, n))


def _moba_prompt(qt, k, vt, k_mean, slopes, b, l):
    e = k.shape[1]
    nb = l // MOBA_BLOCK
    n_sel = min(MOBA_TOPK, (l - 1) // MOBA_BLOCK)
    npair = e // LANES
    return pl.pallas_call(
        functools.partial(_moba_prompt_kernel, n_sel),
        grid=(b, npair, nb),
        in_specs=[pl.BlockSpec(memory_space=pltpu.SMEM),
                  pl.BlockSpec((1, LANES, MOBA_BLOCK), lambda bi, p, i: (bi, p, i)),
                  pl.BlockSpec((l, LANES), lambda bi, p, i: (bi, p)),
                  pl.BlockSpec((1, LANES, l), lambda bi, p, i: (bi, p, 0)),
                  pl.BlockSpec((1, nb, LANES), lambda bi, p, i: (bi, 0, p))],
        out_specs=pl.BlockSpec((MOBA_BLOCK, LANES), lambda bi, p, i: (bi * nb + i, p)),
        out_shape=jax.ShapeDtypeStruct((b * l, e), BF16),
        compiler_params=_params("parallel", "parallel", "arbitrary"),
        name="moba_prompt",
    )(slopes, qt, k, vt, k_mean)


def _moba_select_kernel(n_sel, ppb, bps, pt_ref, q_ref, *refs):
    del pt_ref
    pages = refs[:ppb * bps]
    idx_ref, qcol_scr, gate_scr = refs[ppb * bps:]
    t = pl.program_id(1)
    e, page = qcol_scr.shape
    hd = e // MOBA_HEADS
    nbf = pl.num_programs(1) * bps
    lane = lax.broadcasted_iota(jnp.int32, (MOBA_HEADS, LANES), 1)

    @pl.when(t == 0)
    def _():
        qcol_scr[...] = jnp.broadcast_to(q_ref[0], (page, e)).T
        gate_scr[...] = jnp.zeros_like(gate_scr)

    gate = gate_scr[...]
    for blk in range(bps):
        tot = pages[blk * ppb][0]
        for j in range(1, ppb):
            tot = tot + pages[blk * ppb + j][0]
        per_head = jnp.sum((tot * qcol_scr[...]).reshape(MOBA_HEADS, hd, page), axis=1)
        g = jnp.sum(per_head, -1, keepdims=True) * (1.0 / MOBA_BLOCK)
        gate = jnp.where(lane == t * bps + blk, g, gate)
    gate_scr[...] = gate

    @pl.when(t == pl.num_programs(1) - 1)
    def _():
        lane_f = lane.astype(F32)
        g = jnp.where(lane < nbf, gate, -jnp.inf)
        picks = jnp.zeros((MOBA_HEADS, LANES), F32)
        for r in range(n_sel):
            best = jnp.max(g, -1, keepdims=True)
            idx = jnp.min(jnp.where(g == best, lane_f, float(LANES)), -1, keepdims=True)
            picks = jnp.where(lane == r, idx, picks)
            g = jnp.where(lane_f == idx, -jnp.inf, g)
        idx_ref[0] = picks.astype(jnp.int32)


def _moba_select(q, cache_kt, page_table, n_sel):
    m, e = q.shape
    _, _, page = cache_kt.shape
    ppb = MOBA_BLOCK // page
    nbf = page_table.shape[1] // ppb
    bps = SUBLANES
    assert nbf % bps == 0 and nbf <= LANES and page == LANES

    def page_spec(j):
        return pl.BlockSpec((1, e, page), lambda bi, t, pt: (pt[bi, t * bps * ppb + j], 0, 0))

    return pl.pallas_call(
        functools.partial(_moba_select_kernel, n_sel, ppb, bps),
        grid_spec=pltpu.PrefetchScalarGridSpec(
            num_scalar_prefetch=1,
            grid=(m, nbf // bps),
            in_specs=[pl.BlockSpec((1, 1, e), lambda bi, t, pt: (bi, 0, 0))]
            + [page_spec(j) for j in range(ppb * bps)],
            out_specs=pl.BlockSpec((1, MOBA_HEADS, LANES), lambda bi, t, pt: (bi, 0, 0)),
            scratch_shapes=[pltpu.VMEM((e, page), F32), pltpu.VMEM((MOBA_HEADS, LANES), F32)]),
        out_shape=jax.ShapeDtypeStruct((m, MOBA_HEADS, LANES), jnp.int32),
        compiler_params=_params("parallel", "arbitrary"),
        name="moba_select",
    )(page_table, q.reshape(m, 1, e), *([cache_kt] * (ppb * bps)))


def _moba_sample_kernel(n_sel, ppb, past, pt_ref, sel_ref, slopes_ref, q_ref, kn_ref, vn_ref, *refs):
    del pt_ref
    ntile = 2 * n_sel * ppb
    k_tiles = refs[:ntile]
    v_tiles = refs[ntile:2 * ntile]
    o_ref = refs[2 * ntile]
    bi = pl.program_id(0)
    p = pl.program_id(1)
    hd, page = k_tiles[0].shape[1:]
    pos_in_page = lax.broadcasted_iota(jnp.int32, (SUBLANES, page), 1)
    q = q_ref[0]
    outs = []
    for hh in range(2):
        ls = slice(hh * hd, (hh + 1) * hd)
        h = 2 * p + hh
        slope = slopes_ref[h]
        qh = q[:, ls] * hd ** -0.5
        q8 = jnp.broadcast_to(qh, (SUBLANES, hd))
        s_own = jnp.sum(qh * kn_ref[0][:, ls], -1, keepdims=True)
        scores = []
        for s in range(n_sel):
            blk = sel_ref[bi, h * n_sel + s]
            for j in range(ppb):
                kt = k_tiles[(hh * n_sel + s) * ppb + j][0]
                dist = (past - blk * MOBA_BLOCK - j * page - pos_in_page).astype(F32)
                scores.append(_bdot(q8, kt) - slope * dist)
        mx = s_own
        for sc in scores:
            mx = jnp.maximum(mx, jnp.max(sc, -1, keepdims=True)[0:1])
        p_own = jnp.exp(s_own - mx)
        den = p_own
        acc = p_own * vn_ref[0][:, ls]
        for n, sc in enumerate(scores):
            pm = jnp.exp(sc - mx)
            den = den + jnp.sum(pm, -1, keepdims=True)[0:1]
            acc = acc + _bdot_nt(pm, v_tiles[hh * n_sel * ppb + n][0])[0:1]
        outs.append(acc / den)
    o_ref[0] = jnp.concatenate(outs, axis=-1)


def _moba_sample(q, k_new, v_new, cache_kt, cache_vt, page_table, sel, slopes, n_sel):
    m, e = q.shape
    _, _, page = cache_kt.shape
    hd = e // MOBA_HEADS
    ppb = MOBA_BLOCK // page
    past = page_table.shape[1] * page
    npair = e // LANES
    r3 = lambda a: a.reshape(m, 1, e)
    tok = pl.BlockSpec((1, 1, LANES), lambda bi, p, pt, sl: (bi, 0, p))

    def tile_spec(hh, s, j):
        def index(bi, p, pt, sl):
            h = 2 * p + hh
            return pt[bi, sl[bi, h * n_sel + s] * ppb + j], h, 0
        return pl.BlockSpec((1, hd, page), index)

    tiles = [tile_spec(hh, s, j) for hh in range(2) for s in range(n_sel) for j in range(ppb)]
    out = pl.pallas_call(
        functools.partial(_moba_sample_kernel, n_sel, ppb, past),
        grid_spec=pltpu.PrefetchScalarGridSpec(
            num_scalar_prefetch=2,
            grid=(m, npair),
            in_specs=[pl.BlockSpec(memory_space=pltpu.SMEM), tok, tok, tok] + tiles + tiles,
            out_specs=pl.BlockSpec((1, 1, LANES), lambda bi, p, pt, sl: (bi, 0, p))),
        out_shape=jax.ShapeDtypeStruct((m, 1, e), F32),
        compiler_params=_params("parallel", "parallel"),
        name="moba_sample",
    )(page_table, sel, slopes, r3(q), r3(k_new), r3(v_new),
      *([cache_kt] * len(tiles)), *([cache_vt] * len(tiles)))
    return out.reshape(m, e)


def _row_tile(m, cap):
    t = min(m, cap)
    assert m % t == 0
    return t


def kernel(x_prompt, x_sample, cache_k, cache_v, state_delta, state_conv, page_table,
           gdn_w_in, gdn_w_conv, gdn_a_log, gdn_dt_bias, gdn_w_onorm, gdn_w_o,
           w_kv, moba_w_q, moba_w_o, mlp_w_up, mlp_w_down, ln_g, ln_b):
    bp, lp, d = x_prompt.shape
    bs, ls, _ = x_sample.shape
    depth = mlp_w_up.shape[0]
    n_a = gdn_w_in.shape[0]
    alpha = (2 * depth) ** 0.25
    n_pool, page = cache_k.shape[0], cache_k.shape[1]
    e_kv = cache_k.shape[2] * cache_k.shape[3]
    hw = gdn_w_o.shape[1]
    conv_ch = gdn_w_conv.shape[2]
    kw = gdn_w_conv.shape[1]
    n_pages = page_table.shape[1]
    assert ls == 1, "sample stream handles one new token per sequence"
    assert conv_ch == 3 * hw and kw - 1 <= SUBLANES and kw >= 3
    assert lp % MOBA_BLOCK == 0 and MOBA_BLOCK % GDN_CHUNK == 0 and MOBA_BLOCK % page == 0
    assert (n_pages * page) % MOBA_BLOCK == 0, "past length must end on a MoBA block boundary"
    assert e_kv == MOBA_HEADS * (LANES // 2)
    assert depth - n_a == 1, "one MoBA layer reads the shared K/V"
    nbf = n_pages * page // MOBA_BLOCK
    n_sel_s = min(MOBA_TOPK, nbf)
    assert n_sel_s >= 1

    xp = x_prompt.reshape(bp * lp, d)
    xs = x_sample.reshape(bs, d)
    tm_p = _row_tile(bp * lp, 512)
    tm_s = _row_tile(bs, 512)
    row = lambda a: a.reshape(1, -1).astype(F32)
    heads = jnp.arange(1, MOBA_HEADS + 1, dtype=F32)
    slopes = jnp.exp2(-ALIBI_MAX_EXP * heads / MOBA_HEADS)
    cache_kt = cache_k.transpose(0, 2, 3, 1).reshape(n_pool, e_kv, page)
    cache_vt = cache_v.transpose(0, 2, 3, 1).reshape(n_pool, e_kv, page)

    conv_p, delta_p, conv_s, delta_s = [], [], [], []
    for layer in range(depth):
        g0, b0 = row(ln_g[layer, 0]), row(ln_b[layer, 0])
        g1, b1 = row(ln_g[layer, 1]), row(ln_b[layer, 1])
        if layer < n_a:
            w_in = gdn_w_in[layer]
            w_qkvz = w_in[:, :4 * hw].astype(BF16)
            w_gate = w_in[:, 4 * hw:]
            w_ab = jnp.pad(w_gate, ((0, 0), (0, LANES - 2 * GDN_HEADS))).astype(BF16)
            w_abt = w_gate.T.astype(BF16)
            pad_r = lambda a: jnp.pad(row(a), ((0, 0), (0, LANES - GDN_HEADS)))
            pad_c = lambda a: jnp.pad(a.astype(F32).reshape(-1, 1), ((0, GDN_HEADS), (0, 0)))
            alog_r, dtb_r = pad_r(gdn_a_log[layer]), pad_r(gdn_dt_bias[layer])
            alog_c, dtb_c = pad_c(gdn_a_log[layer]), pad_c(gdn_dt_bias[layer])
            w_conv = gdn_w_conv[layer].astype(F32)
            w_onorm = row(gdn_w_onorm[layer])
            w_o = gdn_w_o[layer].astype(BF16)

            q, k, v, z, gcol, grow, cst = _gdn_in(
                x_prompt if layer == 0 else xp.reshape(bp, lp, d),
                w_qkvz, w_ab, w_abt, w_conv, alog_r, dtb_r, alog_c, dtb_c, tm=2 * MOBA_BLOCK)
            o, s_fin = _gdn_chunks(q, k, v, z, gcol, grow, w_onorm, bp, lp, cpb=4)
            conv_p.append(cst)
            delta_p.append(s_fin)
            mix_p = (o, w_o)

            q, k, v, z, gate, cst = _gdn_in_sample(
                xs, w_qkvz, w_ab, w_conv, alog_r, dtb_r,
                state_conv[layer].astype(F32).transpose(1, 0, 2))
            o, s_new = _gdn_step(q, k, v, z, gate, w_onorm, state_delta[layer].astype(F32))
            conv_s.append(cst.transpose(1, 0, 2))
            delta_s.append(s_new)
            mix_s = (o, w_o)
        else:
            w_q = moba_w_q[layer - n_a]
            k_row, kt_p, vt_p, qt_p, km_p = _kvq_prompt(
                xp, w_kv[:, :e_kv].astype(BF16), w_kv[:, e_kv:].T.astype(BF16),
                w_q.T.astype(BF16), bp, lp)
            k_s, v_s, q_s = _kvq(xs, jnp.concatenate([w_kv, w_q], axis=1).astype(BF16), tm_s)
            w_o = moba_w_o[layer - n_a].astype(BF16)
            o = _moba_prompt(qt_p, k_row, vt_p, km_p.reshape(bp, lp // MOBA_BLOCK, e_kv),
                             slopes, bp, lp)
            mix_p = (o, w_o)
            sel = _moba_select(q_s, cache_kt, page_table, n_sel_s)
            sel = sel[:, :, :n_sel_s].reshape(bs, MOBA_HEADS * n_sel_s)
            o = _moba_sample(q_s, k_s, v_s, cache_kt, cache_vt, page_table, sel, slopes, n_sel_s)
            mix_s = (o, w_o)
        w_up = mlp_w_up[layer].astype(BF16)
        w_down = mlp_w_down[layer].astype(BF16)
        xp = _proj_ln(mix_p[0], xp, mix_p[1], g0, b0, alpha, tm_p)
        xs = _proj_ln(mix_s[0], xs, mix_s[1], g0, b0, alpha, tm_s)
        xp = _mlp_ln(xp, w_up, w_down, g1, b1, alpha, tm_p, 1024)
        xs = _mlp_ln(xs, w_up, w_down, g1, b1, alpha, tm_s, 1024)

    hd = e_kv // MOBA_HEADS
    tokens_major = lambda a: a.reshape(bp, MOBA_HEADS, hd, lp).transpose(0, 3, 1, 2)
    return (xp.reshape(bp, lp, d), xs.reshape(bs, ls, d),
            tokens_major(kt_p), tokens_major(vt_p),
            jnp.stack(delta_p), jnp.stack(conv_p),
            k_s.reshape(bs, ls, MOBA_HEADS, hd), v_s.reshape(bs, ls, MOBA_HEADS, hd),
            jnp.stack(delta_s), jnp.stack(conv_s))
```

```python
import functools
import math

import jax
import jax.numpy as jnp
from jax import lax
from jax.experimental import pallas as pl
from jax.experimental.pallas import tpu as pltpu

F32 = jnp.float32
BF16 = jnp.bfloat16

GDN_HEADS = 8
GDN_CHUNK = 64
MOBA_HEADS = 16
MOBA_BLOCK = 256
MOBA_TOPK = 3
ALIBI_MAX_EXP = 16
LN_EPS = 1e-5
RMS_EPS = 1e-6

LANES = 128
SUBLANES = 8
VMEM_LIMIT_BYTES = 48 * 1024 * 1024

NEG = -1e30
LOG2E = math.log2(math.e)


def _params(*semantics):
    return pltpu.CompilerParams(dimension_semantics=semantics,
                                vmem_limit_bytes=VMEM_LIMIT_BYTES)


def _bdot(a, b):
    return jnp.dot(a.astype(BF16), b.astype(BF16), preferred_element_type=F32)


def _bdot_nt(a, b):
    return lax.dot_general(a.astype(BF16), b.astype(BF16), (((1,), (1,)), ((), ())),
                           preferred_element_type=F32)


def _bdot_tn(a, b):
    return lax.dot_general(a.astype(BF16), b.astype(BF16), (((0,), (0,)), ((), ())),
                           preferred_element_type=F32)


def _split3(a):
    hi = a.astype(BF16)
    r = a - hi.astype(F32)
    mid = r.astype(BF16)
    lo = (r - mid.astype(F32)).astype(BF16)
    return hi, mid, lo


def _layer_norm(h, g, b):
    mu = jnp.mean(h, -1, keepdims=True)
    d = h - mu
    var = jnp.mean(d * d, -1, keepdims=True)
    return d * lax.rsqrt(var + LN_EPS) * g + b


def _silu(x):
    return x * jax.nn.sigmoid(x)


def _softplus(x):
    return jnp.maximum(x, 0.0) + jnp.log(1.0 + jnp.exp(-jnp.abs(x)))


def _chunk_cumsum(g, axis):
    pos = lax.broadcasted_iota(jnp.int32, g.shape, axis) % GDN_CHUNK
    sh = 1
    while sh < GDN_CHUNK:
        g = g + jnp.where(pos >= sh, pltpu.roll(g, sh, axis), 0.0)
        sh *= 2
    return g


def _proj_ln_kernel(alpha, a_ref, x_ref, w_ref, g_ref, b_ref, o_ref):
    sub = jnp.dot(a_ref[...].astype(BF16), w_ref[...], preferred_element_type=F32)
    o_ref[...] = _layer_norm(alpha * x_ref[...] + sub, g_ref[...], b_ref[...])


def _proj_ln(a, x, w, g, b, alpha, tm):
    m, d = x.shape
    e = a.shape[1]
    row = lambda i: (i, 0)
    fixed = lambda i: (0, 0)
    return pl.pallas_call(
        functools.partial(_proj_ln_kernel, alpha),
        grid=(m // tm,),
        in_specs=[pl.BlockSpec((tm, e), row), pl.BlockSpec((tm, d), row),
                  pl.BlockSpec((e, d), fixed), pl.BlockSpec((1, d), fixed),
                  pl.BlockSpec((1, d), fixed)],
        out_specs=pl.BlockSpec((tm, d), row),
        out_shape=jax.ShapeDtypeStruct((m, d), F32),
        compiler_params=_params("parallel"),
        name="proj_ln",
    )(a, x, w, g, b)


def _mlp_kernel(alpha, x_ref, wu_ref, wd_ref, g_ref, b_ref, o_ref, acc_ref):
    f = pl.program_id(1)
    tm = x_ref.shape[0]
    ngrp = 2 if tm % (2 * 2 * SUBLANES) == 0 else 1
    grp = [slice(r * tm // ngrp, (r + 1) * tm // ngrp) for r in range(ngrp)]

    @pl.when(f == 0)
    def _():
        acc_ref[...] = jnp.zeros_like(acc_ref)

    xb = [x_ref[g, :].astype(BF16) for g in grp]
    h = [jnp.dot(xb[r], wu_ref[...], preferred_element_type=F32) for r in range(ngrp)]
    h = [jnp.square(jnp.maximum(h[r], 0.0)).astype(BF16) for r in range(ngrp)]
    part = [jnp.dot(h[r], wd_ref[...], preferred_element_type=F32) for r in range(ngrp)]
    for r in range(ngrp):
        acc_ref[grp[r], :] += part[r]

    @pl.when(f == pl.num_programs(1) - 1)
    def _():
        o_ref[...] = _layer_norm(alpha * x_ref[...] + acc_ref[...], g_ref[...], b_ref[...])


def _mlp_ln(x, w_up, w_down, g, b, alpha, tm, tf):
    m, d = x.shape
    ff = w_up.shape[1]
    return pl.pallas_call(
        functools.partial(_mlp_kernel, alpha),
        grid=(m // tm, ff // tf),
        in_specs=[pl.BlockSpec((tm, d), lambda i, f: (i, 0)),
                  pl.BlockSpec((d, tf), lambda i, f: (0, f)),
                  pl.BlockSpec((tf, d), lambda i, f: (f, 0)),
                  pl.BlockSpec((1, d), lambda i, f: (0, 0)),
                  pl.BlockSpec((1, d), lambda i, f: (0, 0))],
        out_specs=pl.BlockSpec((tm, d), lambda i, f: (i, 0)),
        out_shape=jax.ShapeDtypeStruct((m, d), F32),
        scratch_shapes=[pltpu.VMEM((tm, d), F32)],
        compiler_params=_params("parallel", "arbitrary"),
        name="mlp_ln",
    )(x, w_up, w_down, g, b)


def _kvq_kernel(x_ref, w_ref, k_ref, v_ref, q_ref):
    e = k_ref.shape[1]
    xb = x_ref[...].astype(BF16)
    k_ref[...] = jnp.dot(xb, w_ref[:, 0:e], preferred_element_type=F32)
    v_ref[...] = jnp.dot(xb, w_ref[:, e:2 * e], preferred_element_type=F32)
    q_ref[...] = jnp.dot(xb, w_ref[:, 2 * e:3 * e], preferred_element_type=F32)


def _kvq(x, w_kvq, tm):
    m, d = x.shape
    e = w_kvq.shape[1] // 3
    row = lambda i: (i, 0)
    return pl.pallas_call(
        _kvq_kernel,
        grid=(m // tm,),
        in_specs=[pl.BlockSpec((tm, d), row), pl.BlockSpec((d, 3 * e), lambda i: (0, 0))],
        out_specs=[pl.BlockSpec((tm, e), row)] * 3,
        out_shape=[jax.ShapeDtypeStruct((m, e), F32)] * 3,
        compiler_params=_params("parallel"),
        name="kvq_proj",
    )(x, w_kvq)


def _kvq_prompt_kernel(x_ref, wk_ref, wvt_ref, wqt_ref, k_ref, kt_ref, vt_ref, qt_ref, km_ref):
    xb = x_ref[...].astype(BF16)
    k = jnp.dot(xb, wk_ref[...], preferred_element_type=F32)
    k_ref[...] = k
    kt_ref[0] = k.T
    vt_ref[0] = _bdot_nt(wvt_ref[...], xb)
    qt_ref[0] = _bdot_nt(wqt_ref[...], xb)
    km_ref[0] = jnp.sum(k, axis=0, keepdims=True) * (1.0 / k.shape[0])


def _kvq_prompt(x, w_k, w_vt, w_qt, b, l):
    m, d = x.shape
    e = w_k.shape[1]
    nb = l // MOBA_BLOCK
    fixed = lambda i, j: (0, 0)
    chan = pl.BlockSpec((1, e, MOBA_BLOCK), lambda i, j: (i, 0, j))
    chan_shape = jax.ShapeDtypeStruct((b, e, l), F32)
    return pl.pallas_call(
        _kvq_prompt_kernel,
        grid=(b, nb),
        in_specs=[pl.BlockSpec((MOBA_BLOCK, d), lambda i, j: (i * nb + j, 0)),
                  pl.BlockSpec((d, e), fixed), pl.BlockSpec((e, d), fixed),
                  pl.BlockSpec((e, d), fixed)],
        out_specs=[pl.BlockSpec((MOBA_BLOCK, e), lambda i, j: (i * nb + j, 0)), chan, chan, chan,
                   pl.BlockSpec((1, 1, e), lambda i, j: (i * nb + j, 0, 0))],
        out_shape=[jax.ShapeDtypeStruct((m, e), F32), chan_shape, chan_shape, chan_shape,
                   jax.ShapeDtypeStruct((m // MOBA_BLOCK, 1, e), F32)],
        compiler_params=_params("parallel", "parallel"),
        name="kvq_prompt",
    )(x, w_k, w_vt, w_qt)


def _gate_terms(ab, a_log, dt_bias):
    return -jnp.exp(a_log) * _softplus(ab + dt_bias)


def _gdn_in_kernel(x_ref, w_ref, wab_ref, wabt_ref, wc_ref, alog_r, dtb_r, alog_c, dtb_c,
                   q_ref, k_ref, v_ref, z_ref, gcol_ref, grow_ref, cst_ref,
                   ext_scr):
    t = pl.program_id(1)
    tm = x_ref.shape[1]
    hw = q_ref.shape[1]
    kw = wc_ref.shape[0]
    dk = hw // GDN_HEADS
    xb = x_ref[0].astype(BF16)

    @pl.when(t == 0)
    def _():
        ext_scr[:, 0:SUBLANES, :] = jnp.zeros((3, SUBLANES, hw), F32)

    outs = (q_ref, k_ref, v_ref)
    for s in range(3):
        cols = slice(s * hw, (s + 1) * hw)
        pre = jnp.dot(xb, w_ref[:, cols], preferred_element_type=F32)
        ext_scr[s, SUBLANES:SUBLANES + tm, :] = pre
        conv = pre * wc_ref[kw - 1:kw, cols]
        for sh in range(1, kw):
            conv = conv + (ext_scr[s, pl.ds(SUBLANES - sh, tm), :]
                           * wc_ref[kw - 1 - sh:kw - sh, cols])
        act = _silu(conv)

        @pl.when(t == pl.num_programs(1) - 1)
        def _():
            cst_ref[0, :, cols] = ext_scr[s, pl.ds(SUBLANES + tm - (kw - 1), kw - 1), :]

        ext_scr[s, 0:SUBLANES, :] = pre[tm - SUBLANES:tm, :]

        if s == 2:
            outs[s][...] = act
        else:
            scale = dk ** -0.5 if s == 0 else 1.0
            for h in range(GDN_HEADS):
                hs = slice(h * dk, (h + 1) * dk)
                sl = act[:, hs]
                ss = jnp.sum(sl * sl, -1, keepdims=True)
                outs[s][:, hs] = sl * (lax.rsqrt(ss + RMS_EPS) * scale)

    z_ref[...] = jnp.dot(xb, w_ref[:, 3 * hw:4 * hw], preferred_element_type=F32)

    ab = jnp.dot(xb, wab_ref[...], preferred_element_type=F32)
    lane = lax.broadcasted_iota(jnp.int32, ab.shape, 1)
    g = jnp.where(lane < GDN_HEADS, _gate_terms(ab, alog_r[...], dtb_r[...]), 0.0)
    gcol_ref[...] = jnp.where(lane < GDN_HEADS, _chunk_cumsum(g, 0), jax.nn.sigmoid(ab))

    abt = _bdot_nt(wabt_ref[...], xb)
    gct = _chunk_cumsum(_gate_terms(abt, alog_c[...], dtb_c[...]), 1)
    for c in range(tm // GDN_CHUNK):
        grow_ref[0, c] = gct[0:GDN_HEADS, c * GDN_CHUNK:(c + 1) * GDN_CHUNK]


def _gdn_in(x, w_qkvz, w_ab, w_abt, w_conv, alog_r, dtb_r, alog_c, dtb_c, tm):
    b, l, d = x.shape
    hw = w_qkvz.shape[1] // 4
    kw = w_conv.shape[0]
    nt = l // tm
    cpt = tm // GDN_CHUNK
    row = lambda i, t: (i * nt + t, 0)
    fixed = lambda i, t: (0, 0)
    act = jax.ShapeDtypeStruct((b * l, hw), F32)
    return pl.pallas_call(
        _gdn_in_kernel,
        grid=(b, nt),
        in_specs=[pl.BlockSpec((1, tm, d), lambda i, t: (i, t, 0)),
                  pl.BlockSpec(w_qkvz.shape, fixed), pl.BlockSpec(w_ab.shape, fixed),
                  pl.BlockSpec(w_abt.shape, fixed), pl.BlockSpec(w_conv.shape, fixed),
                  pl.BlockSpec(alog_r.shape, fixed), pl.BlockSpec(dtb_r.shape, fixed),
                  pl.BlockSpec(alog_c.shape, fixed), pl.BlockSpec(dtb_c.shape, fixed)],
        out_specs=[pl.BlockSpec((tm, hw), row)] * 4 + [
            pl.BlockSpec((tm, LANES), row),
            pl.BlockSpec((1, cpt, GDN_HEADS, GDN_CHUNK), lambda i, t: (i, t, 0, 0)),
            pl.BlockSpec((1, kw - 1, 3 * hw), lambda i, t: (i, 0, 0))],
        out_shape=[act] * 4 + [
            jax.ShapeDtypeStruct((b * l, LANES), F32),
            jax.ShapeDtypeStruct((b, l // GDN_CHUNK, GDN_HEADS, GDN_CHUNK), F32),
            jax.ShapeDtypeStruct((b, kw - 1, 3 * hw), F32)],
        scratch_shapes=[pltpu.VMEM((3, SUBLANES + tm, hw), F32)],
        compiler_params=_params("parallel", "arbitrary"),
        name="gdn_in",
    )(x, w_qkvz, w_ab, w_abt, w_conv, alog_r, dtb_r, alog_c, dtb_c)


def _gdn_chunk_kernel(q_ref, k_ref, v_ref, z_ref, gcol_ref, grow_ref, wn_ref,
                      o_ref, sfin_ref, s_scr):
    c = pl.program_id(1)
    cs = GDN_CHUNK
    dk = q_ref.shape[1] // GDN_HEADS

    @pl.when(c == 0)
    def _():
        s_scr[...] = jnp.zeros_like(s_scr)

    ri = lax.broadcasted_iota(jnp.int32, (cs, cs), 0)
    ci = lax.broadcasted_iota(jnp.int32, (cs, cs), 1)
    incl = ri >= ci
    strict = ri > ci
    eye = (ri == ci).astype(F32)
    heads = range(GDN_HEADS)
    cpb = q_ref.shape[0] // cs
    units = [(j, h) for j in range(cpb) for h in heads]
    rows = {j: slice(j * cs, (j + 1) * cs) for j in range(cpb)}
    cols = {h: slice(h * dk, (h + 1) * dk) for h in heads}
    each = lambda f: {unit: f(*unit) for unit in units}
    gcol = {j: gcol_ref[rows[j], :] for j in range(cpb)}
    grow = {j: grow_ref[0, j] for j in range(cpb)}
    q = each(lambda j, h: q_ref[rows[j], cols[h]])
    k = each(lambda j, h: k_ref[rows[j], cols[h]])
    qb = each(lambda j, h: q[j, h].astype(BF16))
    kbf = each(lambda j, h: k[j, h].astype(BF16))
    gcc = each(lambda j, h: gcol[j][:, h:h + 1])
    beta = each(lambda j, h: gcol[j][:, GDN_HEADS + h:GDN_HEADS + h + 1])
    decay = each(lambda j, h: jnp.exp(jnp.where(incl, gcc[j, h] - grow[j][h:h + 1, :], NEG)))
    eg = each(lambda j, h: jnp.exp(gcc[j, h]))
    glast = each(lambda j, h: gcc[j, h][cs - 1:cs, :])
    kb = each(lambda j, h: k[j, h] * beta[j, h])
    kk = each(lambda j, h: _bdot_nt(kb[j, h], kbf[j, h]))
    qk = each(lambda j, h: _bdot_nt(qb[j, h], kbf[j, h]))
    p = each(lambda j, h: jnp.where(strict, -kk[j, h] * decay[j, h], 0.0))
    tinv = each(lambda j, h: eye + p[j, h])
    for _ in range(int(math.log2(cs)) - 1):
        pb = each(lambda j, h: p[j, h].astype(BF16))
        p = each(lambda j, h: _bdot(pb[j, h], pb[j, h]))
        tinv = each(lambda j, h: tinv[j, h] + _bdot(tinv[j, h], p[j, h]))
    tb = each(lambda j, h: tinv[j, h].astype(BF16))
    u = each(lambda j, h: _bdot(tb[j, h], v_ref[rows[j], cols[h]] * beta[j, h]))
    w = each(lambda j, h: _bdot(tb[j, h], kb[j, h] * eg[j, h]).astype(BF16))
    qg = each(lambda j, h: (q[j, h] * eg[j, h]).astype(BF16))
    attn = each(lambda j, h: (qk[j, h] * decay[j, h]).astype(BF16))
    ktail = each(lambda j, h: (k[j, h] * jnp.exp(glast[j, h] - gcc[j, h])).astype(BF16))
    s = [s_scr[h] for h in heads]
    for j in range(cpb):
        sb = [s[h].astype(BF16) for h in heads]
        v_new = [(u[j, h] - _bdot(w[j, h], sb[h])).astype(BF16) for h in heads]
        o = [_bdot(qg[j, h], sb[h]) + _bdot(attn[j, h], v_new[h]) for h in heads]
        s = [s[h] * jnp.exp(glast[j, h]) + _bdot_tn(ktail[j, h], v_new[h]) for h in heads]
        for h in heads:
            on = o[h] * lax.rsqrt(jnp.mean(o[h] * o[h], -1, keepdims=True) + RMS_EPS) * wn_ref[...]
            o_ref[rows[j], cols[h]] = (on * _silu(z_ref[rows[j], cols[h]])).astype(o_ref.dtype)
    for h in heads:
        s_scr[h] = s[h]

    @pl.when(c == pl.num_programs(1) - 1)
    def _():
        sfin_ref[0] = s_scr[...]


def _gdn_chunks(q, k, v, z, gcol, grow, w_onorm, b, l, cpb):
    hw = q.shape[1]
    dk = hw // GDN_HEADS
    nc = l // (cpb * GDN_CHUNK)
    tr = cpb * GDN_CHUNK
    row = lambda i, c: (i * nc + c, 0)
    return pl.pallas_call(
        _gdn_chunk_kernel,
        grid=(b, nc),
        in_specs=[pl.BlockSpec((tr, hw), row)] * 4 + [
            pl.BlockSpec((tr, LANES), row),
            pl.BlockSpec((1, cpb, GDN_HEADS, GDN_CHUNK), lambda i, c: (i, c, 0, 0)),
            pl.BlockSpec((1, dk), lambda i, c: (0, 0))],
        out_specs=[pl.BlockSpec((tr, hw), row),
                   pl.BlockSpec((1, GDN_HEADS, dk, dk), lambda i, c: (i, 0, 0, 0))],
        out_shape=[jax.ShapeDtypeStruct((b * l, hw), BF16),
                   jax.ShapeDtypeStruct((b, GDN_HEADS, dk, dk), F32)],
        scratch_shapes=[pltpu.VMEM((GDN_HEADS, dk, dk), F32)],
        compiler_params=_params("parallel", "arbitrary"),
        name="gdn_chunks",
    )(q, k, v, z, gcol, grow, w_onorm)


def _gdn_in_sample_kernel(x_ref, w_ref, wab_ref, wc_ref, alog_r, dtb_r, sc_ref,
                          q_ref, k_ref, v_ref, z_ref, gate_ref, cst_ref):
    hw = q_ref.shape[1]
    kw = wc_ref.shape[0]
    dk = hw // GDN_HEADS
    xb = x_ref[...].astype(BF16)
    outs = (q_ref, k_ref, v_ref)
    for s in range(3):
        cols = slice(s * hw, (s + 1) * hw)
        pre = jnp.dot(xb, w_ref[:, cols], preferred_element_type=F32)
        conv = pre * wc_ref[kw - 1:kw, cols]
        for j in range(kw - 1):
            conv = conv + sc_ref[j, :, cols] * wc_ref[j:j + 1, cols]
        cst_ref[kw - 2, :, cols] = pre
        act = _silu(conv)
        if s == 2:
            outs[s][...] = act
        else:
            scale = dk ** -0.5 if s == 0 else 1.0
            for h in range(GDN_HEADS):
                hs = slice(h * dk, (h + 1) * dk)
                sl = act[:, hs]
                ss = jnp.sum(sl * sl, -1, keepdims=True)
                outs[s][:, hs] = sl * (lax.rsqrt(ss + RMS_EPS) * scale)
    cst_ref[0:kw - 2] = sc_ref[1:kw - 1]
    z_ref[...] = jnp.dot(xb, w_ref[:, 3 * hw:4 * hw], preferred_element_type=F32)
    ab = jnp.dot(xb, wab_ref[...], preferred_element_type=F32)
    lane = lax.broadcasted_iota(jnp.int32, ab.shape, 1)
    gate_ref[...] = jnp.where(lane < GDN_HEADS, _gate_terms(ab, alog_r[...], dtb_r[...]),
                              jax.nn.sigmoid(ab))


def _gdn_in_sample(x, w_qkvz, w_ab, w_conv, alog_r, dtb_r, conv_state):
    m = x.shape[0]
    hw = w_qkvz.shape[1] // 4
    act = jax.ShapeDtypeStruct((m, hw), F32)
    return pl.pallas_call(
        _gdn_in_sample_kernel,
        out_shape=[act] * 4 + [jax.ShapeDtypeStruct((m, LANES), F32),
                               jax.ShapeDtypeStruct(conv_state.shape, F32)],
        compiler_params=pltpu.CompilerParams(vmem_limit_bytes=VMEM_LIMIT_BYTES),
        name="gdn_in_sample",
    )(x, w_qkvz, w_ab, w_conv, alog_r, dtb_r, conv_state)


def _gdn_step_kernel(q_ref, k_ref, v_ref, z_ref, gate_ref, wn_ref, s_ref, o_ref, snew_ref):
    dk = s_ref.shape[2]
    gate = gate_ref[0]
    for h in range(GDN_HEADS):
        hs = slice(h * dk, (h + 1) * dk)
        q = q_ref[0, :, hs]
        k = k_ref[0, :, hs]
        v = v_ref[0, :, hs]
        eg = jnp.exp(gate[:, h:h + 1])
        beta = gate[:, GDN_HEADS + h:GDN_HEADS + h + 1]
        s = s_ref[0, h]
        kc = jnp.broadcast_to(k, (dk, dk)).T
        qc = jnp.broadcast_to(q, (dk, dk)).T
        ks = jnp.sum(kc * s, axis=0, keepdims=True)
        qs = jnp.sum(qc * s, axis=0, keepdims=True)
        v_new = beta * (v - eg * ks)
        o = eg * qs + jnp.sum(q * k, -1, keepdims=True) * v_new
        snew_ref[0, h] = eg * s + kc * v_new
        o = o * lax.rsqrt(jnp.mean(o * o, -1, keepdims=True) + RMS_EPS) * wn_ref[...]
        o_ref[0, :, hs] = o * _silu(z_ref[0, :, hs])


def _gdn_step(q, k, v, z, gate, w_onorm, state):
    m, hw = q.shape
    dk = hw // GDN_HEADS
    r3 = lambda a: a.reshape(m, 1, a.shape[1])
    tok = pl.BlockSpec((1, 1, hw), lambda i: (i, 0, 0))
    st = pl.BlockSpec((1, GDN_HEADS, dk, dk), lambda i: (i, 0, 0, 0))
    o, s_new = pl.pallas_call(
        _gdn_step_kernel,
        grid=(m,),
        in_specs=[tok] * 4 + [pl.BlockSpec((1, 1, LANES), lambda i: (i, 0, 0)),
                              pl.BlockSpec((1, dk), lambda i: (0, 0)), st],
        out_specs=[tok, st],
        out_shape=[jax.ShapeDtypeStruct((m, 1, hw), F32),
                   jax.ShapeDtypeStruct(state.shape, F32)],
        compiler_params=_params("parallel"),
        name="gdn_step",
    )(r3(q), r3(k), r3(v), r3(z), r3(gate), w_onorm, state)
    return o.reshape(m, hw), s_new


def _select_step(n_sel, ppb, nbf, t, q_ref, pages, idx_ref, qcol_scr, gate_scr):
    e, page = qcol_scr.shape
    hd = e // MOBA_HEADS
    bps = len(pages) // ppb
    lane = lax.broadcasted_iota(jnp.int32, (MOBA_HEADS, LANES), 1)

    @pl.when(t == 0)
    def _():
        qcol_scr[...] = jnp.broadcast_to(q_ref[0], (page, e)).T
        gate_scr[...] = jnp.zeros_like(gate_scr)

    gate = gate_scr[...]
    for blk in range(bps):
        tot = pages[blk * ppb][0]
        for j in range(1, ppb):
            tot = tot + pages[blk * ppb + j][0]
        per_head = jnp.sum((tot * qcol_scr[...]).reshape(MOBA_HEADS, hd, page), axis=1)
        g = jnp.sum(per_head, -1, keepdims=True) * (1.0 / MOBA_BLOCK)
        gate = jnp.where(lane == t * bps + blk, g, gate)
    gate_scr[...] = gate

    @pl.when(t == nbf // bps - 1)
    def _():
        lane_f = lane.astype(F32)
        g = jnp.where(lane < nbf, gate, -jnp.inf)
        picks = jnp.zeros((MOBA_HEADS, LANES), F32)
        for r in range(n_sel):
            best = jnp.max(g, -1, keepdims=True)
            idx = jnp.min(jnp.where(g == best, lane_f, float(LANES)), -1, keepdims=True)
            picks = jnp.where(lane == r, idx, picks)
            g = jnp.where(lane_f == idx, -jnp.inf, g)
        idx_ref[0] = picks.astype(jnp.int32)


def _moba_prompt_kernel(n_sel, n_sel_s, ppb, nbf, pt_ref, slopes_ref, qt_ref, k_ref, vt_ref,
                        km_ref, qs_ref, *rest):
    del pt_ref
    pages = rest[:-4]
    o_ref, idx_ref, qcol_scr, gate_scr = rest[-4:]
    p = pl.program_id(1)
    i = pl.program_id(2)
    step = (pl.program_id(0) * pl.num_programs(1) + p) * pl.num_programs(2) + i
    _select_step(n_sel_s, ppb, nbf, step % (nbf * ppb // len(pages)), qs_ref, pages, idx_ref,
                 qcol_scr, gate_scr)
    bs = MOBA_BLOCK
    hd = LANES // 2
    nb = km_ref.shape[1]
    nbp = -(-nb // SUBLANES) * SUBLANES
    qt = qt_ref[0]
    km = km_ref[0]
    if nbp > nb:
        km = jnp.concatenate([km, jnp.zeros((nbp - nb, LANES), F32)], axis=0)
    k_hi, k_mid, _ = _split3(km)
    chan = lax.broadcasted_iota(jnp.int32, (LANES, bs), 0)
    blk = lax.broadcasted_iota(jnp.int32, (nbp, bs), 0)
    rel = (lax.broadcasted_iota(jnp.int32, (bs, bs), 1)
           - lax.broadcasted_iota(jnp.int32, (bs, bs), 0))
    q_idx = lax.broadcasted_iota(jnp.int32, (1, bs), 1).astype(F32)
    aug_row = lax.broadcasted_iota(jnp.int32, (2 * SUBLANES, bs), 0)
    aug_lane = lax.broadcasted_iota(jnp.int32, (bs, LANES), 1)
    k_idx = lax.broadcasted_iota(jnp.int32, (bs, LANES), 0).astype(F32)
    k_aug = jnp.where(aug_lane < 3, k_idx, jnp.where(aug_lane < 6, 1.0, 0.0)).astype(BF16)
    wqs, keeps, slope_h = [], [], []
    for hh in range(2):
        qh = jnp.where((chan // hd) == hh, qt, 0.0)
        q_hi, q_mid, _ = _split3(qh)
        gate = _bdot(k_hi, q_hi) + _bdot(k_hi, q_mid) + _bdot(k_mid, q_hi)
        cnt = jnp.zeros((nbp, bs), jnp.int32)
        for m in range(nb):
            gm = gate[m:m + 1, :]
            beats = (gm > gate) | ((gm == gate) & (m < blk))
            cnt = cnt + jnp.where(beats, jnp.where(m < i, 1, 0), 0)
        keeps.append((blk < i) & (cnt < n_sel))
        slope2 = slopes_ref[2 * p + hh] * LOG2E
        slope_h.append(slope2)
        s_parts = _split3(jnp.full((1, bs), slope2, F32))
        t_parts = _split3(-slope2 * q_idx)
        aug = jnp.zeros((2 * SUBLANES, bs), F32)
        for r, part in enumerate(s_parts + t_parts):
            aug = jnp.where(aug_row == r, part.astype(F32), aug)
        wqs.append(jnp.concatenate(
            [(qh * (hd ** -0.5 * LOG2E)).astype(BF16), aug.astype(BF16),
             jnp.zeros((LANES - 2 * SUBLANES, bs), BF16)], axis=0))

    def attend(n):
        pair = range(2)
        kcat = jnp.concatenate([k_ref[0:n * bs, :].astype(BF16),
                                jnp.concatenate([k_aug] * n, axis=0)], axis=1)
        s_all = [jnp.dot(kcat, wqs[hh], preferred_element_type=F32) for hh in pair]
        us, shifts = [], []
        for hh in pair:
            u = [s_all[hh][j * bs:(j + 1) * bs, :] for j in range(n - 1)]
            u.append(jnp.where(rel >= 0, s_all[hh][(n - 1) * bs:n * bs, :], NEG))
            rows = [jnp.where(keeps[hh][j:j + 1, :], -slope_h[hh] * float((n - 1 - j) * bs), NEG)
                    for j in range(n - 1)] + [jnp.zeros((1, bs), F32)]
            mx = functools.reduce(
                jnp.maximum, [jnp.max(u[j], 0, keepdims=True) + rows[j] for j in range(n)])
            us.append(u)
            shifts.append([mx - rows[j] for j in range(n)])
        ps = [[jnp.exp2(us[hh][j] - shifts[hh][j]) for j in range(n)] for hh in pair]
        den = [sum(jnp.sum(pm, 0, keepdims=True) for pm in ps[hh]) for hh in pair]
        pcat = [jnp.concatenate([pm.astype(BF16) for pm in ps[hh]], axis=0) for hh in pair]
        acc = [_bdot(vt_ref[0, hh * hd:(hh + 1) * hd, 0:n * bs], pcat[hh]) for hh in pair]
        o_ref[...] = jnp.concatenate([acc[hh] / den[hh] for hh in pair],
                                     axis=0).T.astype(o_ref.dtype)

    for n in range(1, nb + 1):
        pl.when(i == n - 1)(functools.partial(attend, n))


def _moba_prompt(qt, k, vt, k_mean, slopes, b, l, q_s, cache_kt, page_table, n_sel_s):
    e = k.shape[1]
    nb = l // MOBA_BLOCK
    n_sel = min(MOBA_TOPK, (l - 1) // MOBA_BLOCK)
    npair = e // LANES
    m = q_s.shape[0]
    _, _, page = cache_kt.shape
    n_pages = page_table.shape[1]
    ppb = MOBA_BLOCK // page
    nbf = n_pages // ppb
    steps = b * npair * nb
    assert steps % m == 0 and nbf <= LANES and page == LANES
    sps = steps // m
    assert n_pages % sps == 0 and (n_pages // sps) % ppb == 0
    pps = n_pages // sps
    seq = lambda bi, p, i: ((bi * npair + p) * nb + i) // sps
    tstep = lambda bi, p, i: ((bi * npair + p) * nb + i) % sps

    def page_spec(j):
        return pl.BlockSpec(
            (1, e, page), lambda bi, p, i, pt: (pt[seq(bi, p, i), tstep(bi, p, i) * pps + j], 0, 0))

    return pl.pallas_call(
        functools.partial(_moba_prompt_kernel, n_sel, n_sel_s, ppb, nbf),
        grid_spec=pltpu.PrefetchScalarGridSpec(
            num_scalar_prefetch=1,
            grid=(b, npair, nb),
            in_specs=[pl.BlockSpec(memory_space=pltpu.SMEM),
                      pl.BlockSpec((1, LANES, MOBA_BLOCK), lambda bi, p, i, pt: (bi, p, i)),
                      pl.BlockSpec((l, LANES), lambda bi, p, i, pt: (bi, p)),
                      pl.BlockSpec((1, LANES, l), lambda bi, p, i, pt: (bi, p, 0)),
                      pl.BlockSpec((1, nb, LANES), lambda bi, p, i, pt: (bi, 0, p)),
                      pl.BlockSpec((1, 1, e), lambda bi, p, i, pt: (seq(bi, p, i), 0, 0))]
            + [page_spec(j) for j in range(pps)],
            out_specs=[pl.BlockSpec((MOBA_BLOCK, LANES), lambda bi, p, i, pt: (bi * nb + i, p)),
                       pl.BlockSpec((1, MOBA_HEADS, LANES),
                                    lambda bi, p, i, pt: (seq(bi, p, i), 0, 0))],
            scratch_shapes=[pltpu.VMEM((e, page), F32), pltpu.VMEM((MOBA_HEADS, LANES), F32)]),
        out_shape=[jax.ShapeDtypeStruct((b * l, e), BF16),
                   jax.ShapeDtypeStruct((m, MOBA_HEADS, LANES), jnp.int32)],
        compiler_params=_params("arbitrary", "arbitrary", "arbitrary"),
        name="moba_prompt",
    )(page_table, slopes, qt, k, vt, k_mean, q_s.reshape(m, 1, e), *([cache_kt] * pps))


def _moba_sample_kernel(n_sel, ppb, past, hps, pt_ref, sel_ref, slopes_ref, q_ref, kn_ref, vn_ref,
                        *refs):
    del pt_ref
    tph = n_sel * ppb
    k_tiles = refs[:hps * tph]
    v_tiles = refs[hps * tph:2 * hps * tph]
    o_ref = refs[2 * hps * tph]
    bi = pl.program_id(0)
    grp = pl.program_id(1)
    hd, page = k_tiles[0].shape[1:]
    pos_in_page = lax.broadcasted_iota(jnp.int32, (SUBLANES, page), 1)
    q = q_ref[0]
    kn = kn_ref[0]
    vn = vn_ref[0]
    heads = range(hps)
    tiles = range(tph)
    ls = [slice(hh * hd, (hh + 1) * hd) for hh in heads]
    qh = [q[:, ls[hh]] * hd ** -0.5 for hh in heads]
    q8 = [jnp.broadcast_to(qh[hh], (SUBLANES, hd)).astype(BF16) for hh in heads]
    s_own = [jnp.sum(qh[hh] * kn[:, ls[hh]], -1, keepdims=True) for hh in heads]
    raw = [[jnp.dot(q8[hh], k_tiles[hh * tph + n][0].astype(BF16), preferred_element_type=F32)
            for n in tiles] for hh in heads]
    scores = []
    for hh in heads:
        h = grp * hps + hh
        slope = slopes_ref[h]
        row = []
        for n in tiles:
            blk = sel_ref[bi, h * n_sel + n // ppb]
            dist = (past - blk * MOBA_BLOCK - (n % ppb) * page - pos_in_page).astype(F32)
            row.append(raw[hh][n] - slope * dist)
        scores.append(row)
    mx = [functools.reduce(jnp.maximum,
                           [jnp.max(sc, -1, keepdims=True)[0:1] for sc in scores[hh]], s_own[hh])
          for hh in heads]
    pm = [[jnp.exp(sc - mx[hh]) for sc in scores[hh]] for hh in heads]
    pv = [[_bdot_nt(pm[hh][n], v_tiles[hh * tph + n][0])[0:1] for n in tiles] for hh in heads]
    outs = []
    for hh in heads:
        p_own = jnp.exp(s_own[hh] - mx[hh])
        den = p_own + sum(jnp.sum(pm[hh][n], -1, keepdims=True)[0:1] for n in tiles)
        acc = p_own * vn[:, ls[hh]] + sum(pv[hh])
        outs.append(acc / den)
    o_ref[0] = jnp.concatenate(outs, axis=-1)


def _moba_sample(q, k_new, v_new, cache_kt, cache_vt, page_table, sel, slopes, n_sel):
    m, e = q.shape
    _, _, page = cache_kt.shape
    hd = e // MOBA_HEADS
    ppb = MOBA_BLOCK // page
    past = page_table.shape[1] * page
    hps = 8
    assert MOBA_HEADS % hps == 0 and (hps * hd) % LANES == 0
    r3 = lambda a: a.reshape(m, 1, e)
    tok = pl.BlockSpec((1, 1, hps * hd), lambda bi, g, pt, sl: (bi, 0, g))

    def tile_spec(hh, s, j):
        def index(bi, g, pt, sl):
            h = g * hps + hh
            return pt[bi, sl[bi, h * n_sel + s] * ppb + j], h, 0
        return pl.BlockSpec((1, hd, page), index)

    tiles = [tile_spec(hh, s, j) for hh in range(hps) for s in range(n_sel) for j in range(ppb)]
    out = pl.pallas_call(
        functools.partial(_moba_sample_kernel, n_sel, ppb, past, hps),
        grid_spec=pltpu.PrefetchScalarGridSpec(
            num_scalar_prefetch=2,
            grid=(m, MOBA_HEADS // hps),
            in_specs=[pl.BlockSpec(memory_space=pltpu.SMEM), tok, tok, tok] + tiles + tiles,
            out_specs=tok),
        out_shape=jax.ShapeDtypeStruct((m, 1, e), F32),
        compiler_params=_params("parallel", "parallel"),
        name="moba_sample",
    )(page_table, sel, slopes, r3(q), r3(k_new), r3(v_new),
      *([cache_kt] * len(tiles)), *([cache_vt] * len(tiles)))
    return out.reshape(m, e)


def _row_tile(m, cap):
    t = min(m, cap)
    assert m % t == 0
    return t


def kernel(x_prompt, x_sample, cache_k, cache_v, state_delta, state_conv, page_table,
           gdn_w_in, gdn_w_conv, gdn_a_log, gdn_dt_bias, gdn_w_onorm, gdn_w_o,
           w_kv, moba_w_q, moba_w_o, mlp_w_up, mlp_w_down, ln_g, ln_b):
    bp, lp, d = x_prompt.shape
    bs, ls, _ = x_sample.shape
    depth = mlp_w_up.shape[0]
    n_a = gdn_w_in.shape[0]
    alpha = (2 * depth) ** 0.25
    n_pool, page = cache_k.shape[0], cache_k.shape[1]
    e_kv = cache_k.shape[2] * cache_k.shape[3]
    hw = gdn_w_o.shape[1]
    conv_ch = gdn_w_conv.shape[2]
    kw = gdn_w_conv.shape[1]
    n_pages = page_table.shape[1]
    assert ls == 1, "sample stream handles one new token per sequence"
    assert conv_ch == 3 * hw and kw - 1 <= SUBLANES and kw >= 3
    assert lp % MOBA_BLOCK == 0 and MOBA_BLOCK % GDN_CHUNK == 0 and MOBA_BLOCK % page == 0
    assert (n_pages * page) % MOBA_BLOCK == 0, "past length must end on a MoBA block boundary"
    assert e_kv == MOBA_HEADS * (LANES // 2)
    assert depth - n_a == 1, "one MoBA layer reads the shared K/V"
    nbf = n_pages * page // MOBA_BLOCK
    n_sel_s = min(MOBA_TOPK, nbf)
    assert n_sel_s >= 1

    xp = x_prompt.reshape(bp * lp, d)
    xs = x_sample.reshape(bs, d)
    tm_p = _row_tile(bp * lp, 512)
    tm_s = _row_tile(bs, 512)
    row = lambda a: a.reshape(1, -1).astype(F32)
    heads = jnp.arange(1, MOBA_HEADS + 1, dtype=F32)
    slopes = jnp.exp2(-ALIBI_MAX_EXP * heads / MOBA_HEADS)
    cache_kt = cache_k.transpose(0, 2, 3, 1).reshape(n_pool, e_kv, page)
    cache_vt = cache_v.transpose(0, 2, 3, 1).reshape(n_pool, e_kv, page)

    conv_p, delta_p, conv_s, delta_s = [], [], [], []
    for layer in range(depth):
        g0, b0 = row(ln_g[layer, 0]), row(ln_b[layer, 0])
        g1, b1 = row(ln_g[layer, 1]), row(ln_b[layer, 1])
        if layer < n_a:
            w_in = gdn_w_in[layer]
            w_qkvz = w_in[:, :4 * hw].astype(BF16)
            w_gate = w_in[:, 4 * hw:]
            w_ab = jnp.pad(w_gate, ((0, 0), (0, LANES - 2 * GDN_HEADS))).astype(BF16)
            w_abt = w_gate.T.astype(BF16)
            pad_r = lambda a: jnp.pad(row(a), ((0, 0), (0, LANES - GDN_HEADS)))
            pad_c = lambda a: jnp.pad(a.astype(F32).reshape(-1, 1), ((0, GDN_HEADS), (0, 0)))
            alog_r, dtb_r = pad_r(gdn_a_log[layer]), pad_r(gdn_dt_bias[layer])
            alog_c, dtb_c = pad_c(gdn_a_log[layer]), pad_c(gdn_dt_bias[layer])
            w_conv = gdn_w_conv[layer].astype(F32)
            w_onorm = row(gdn_w_onorm[layer])
            w_o = gdn_w_o[layer].astype(BF16)

            q, k, v, z, gcol, grow, cst = _gdn_in(
                x_prompt if layer == 0 else xp.reshape(bp, lp, d),
                w_qkvz, w_ab, w_abt, w_conv, alog_r, dtb_r, alog_c, dtb_c, tm=2 * MOBA_BLOCK)
            o, s_fin = _gdn_chunks(q, k, v, z, gcol, grow, w_onorm, bp, lp, cpb=4)
            conv_p.append(cst)
            delta_p.append(s_fin)
            mix_p = (o, w_o)

            q, k, v, z, gate, cst = _gdn_in_sample(
                xs, w_qkvz, w_ab, w_conv, alog_r, dtb_r,
                state_conv[layer].astype(F32).transpose(1, 0, 2))
            o, s_new = _gdn_step(q, k, v, z, gate, w_onorm, state_delta[layer].astype(F32))
            conv_s.append(cst.transpose(1, 0, 2))
            delta_s.append(s_new)
            mix_s = (o, w_o)
        else:
            w_q = moba_w_q[layer - n_a]
            k_row, kt_p, vt_p, qt_p, km_p = _kvq_prompt(
                xp, w_kv[:, :e_kv].astype(BF16), w_kv[:, e_kv:].T.astype(BF16),
                w_q.T.astype(BF16), bp, lp)
            k_s, v_s, q_s = _kvq(xs, jnp.concatenate([w_kv, w_q], axis=1).astype(BF16), tm_s)
            w_o = moba_w_o[layer - n_a].astype(BF16)
            o, sel = _moba_prompt(qt_p, k_row, vt_p, km_p.reshape(bp, lp // MOBA_BLOCK, e_kv),
                                  slopes, bp, lp, q_s, cache_kt, page_table, n_sel_s)
            mix_p = (o, w_o)
            sel = sel[:, :, :n_sel_s].reshape(bs, MOBA_HEADS * n_sel_s)
            o = _moba_sample(q_s, k_s, v_s, cache_kt, cache_vt, page_table, sel, slopes, n_sel_s)
            mix_s = (o, w_o)
        w_up = mlp_w_up[layer].astype(BF16)
        w_down = mlp_w_down[layer].astype(BF16)
        xp = _proj_ln(mix_p[0], xp, mix_p[1], g0, b0, alpha, tm_p)
        xs = _proj_ln(mix_s[0], xs, mix_s[1], g0, b0, alpha, tm_s)
        xp = _mlp_ln(xp, w_up, w_down, g1, b1, alpha, tm_p, 1024)
        xs = _mlp_ln(xs, w_up, w_down, g1, b1, alpha, tm_s, 1024)

    hd = e_kv // MOBA_HEADS
    tokens_major = lambda a: a.reshape(bp, MOBA_HEADS, hd, lp).transpose(0, 3, 1, 2)
    return (xp.reshape(bp, lp, d), xs.reshape(bs, ls, d),
            tokens_major(kt_p), tokens_major(vt_p),
            jnp.stack(delta_p), jnp.stack(conv_p),
            k_s.reshape(bs, ls, MOBA_HEADS, hd), v_s.reshape(bs, ls, MOBA_HEADS, hd),
            jnp.stack(delta_s), jnp.stack(conv_s))
```

```python
import functools
import math

import jax
import jax.numpy as jnp
from jax import lax
from jax.experimental import pallas as pl
from jax.experimental.pallas import tpu as pltpu

F32 = jnp.float32
BF16 = jnp.bfloat16

GDN_HEADS = 8
GDN_CHUNK = 64
MOBA_HEADS = 16
MOBA_BLOCK = 256
MOBA_TOPK = 3
ALIBI_MAX_EXP = 16
LN_EPS = 1e-5
RMS_EPS = 1e-6

LANES = 128
SUBLANES = 8
VMEM_LIMIT_BYTES = 48 * 1024 * 1024

NEG = -1e30
LOG2E = math.log2(math.e)


def _params(*semantics):
    return pltpu.CompilerParams(dimension_semantics=semantics,
                                vmem_limit_bytes=VMEM_LIMIT_BYTES)


def _bdot(a, b):
    return jnp.dot(a.astype(BF16), b.astype(BF16), preferred_element_type=F32)


def _bdot_nt(a, b):
    return lax.dot_general(a.astype(BF16), b.astype(BF16), (((1,), (1,)), ((), ())),
                           preferred_element_type=F32)


def _bdot_tn(a, b):
    return lax.dot_general(a.astype(BF16), b.astype(BF16), (((0,), (0,)), ((), ())),
                           preferred_element_type=F32)


def _split3(a):
    hi = a.astype(BF16)
    r = a - hi.astype(F32)
    mid = r.astype(BF16)
    lo = (r - mid.astype(F32)).astype(BF16)
    return hi, mid, lo


def _layer_norm(h, g, b):
    mu = jnp.mean(h, -1, keepdims=True)
    d = h - mu
    var = jnp.mean(d * d, -1, keepdims=True)
    return d * lax.rsqrt(var + LN_EPS) * g + b


def _silu(x):
    return x * jax.nn.sigmoid(x)


def _softplus(x):
    return jnp.maximum(x, 0.0) + jnp.log(1.0 + jnp.exp(-jnp.abs(x)))


def _chunk_cumsum(g, axis):
    pos = lax.broadcasted_iota(jnp.int32, g.shape, axis) % GDN_CHUNK
    sh = 1
    while sh < GDN_CHUNK:
        g = g + jnp.where(pos >= sh, pltpu.roll(g, sh, axis), 0.0)
        sh *= 2
    return g


def _proj_ln_kernel(alpha, a_ref, x_ref, w_ref, g_ref, b_ref, o_ref):
    sub = jnp.dot(a_ref[...].astype(BF16), w_ref[...], preferred_element_type=F32)
    o_ref[...] = _layer_norm(alpha * x_ref[...] + sub, g_ref[...], b_ref[...])


def _proj_ln(a, x, w, g, b, alpha, tm):
    m, d = x.shape
    e = a.shape[1]
    row = lambda i: (i, 0)
    fixed = lambda i: (0, 0)
    return pl.pallas_call(
        functools.partial(_proj_ln_kernel, alpha),
        grid=(m // tm,),
        in_specs=[pl.BlockSpec((tm, e), row), pl.BlockSpec((tm, d), row),
                  pl.BlockSpec((e, d), fixed), pl.BlockSpec((1, d), fixed),
                  pl.BlockSpec((1, d), fixed)],
        out_specs=pl.BlockSpec((tm, d), row),
        out_shape=jax.ShapeDtypeStruct((m, d), F32),
        compiler_params=_params("parallel"),
        name="proj_ln",
    )(a, x, w, g, b)


def _mlp_kernel(alpha, x_ref, wu_ref, wd_ref, g_ref, b_ref, o_ref, acc_ref):
    f = pl.program_id(1)
    tm = x_ref.shape[0]
    ngrp = 2 if tm % (2 * 2 * SUBLANES) == 0 else 1
    grp = [slice(r * tm // ngrp, (r + 1) * tm // ngrp) for r in range(ngrp)]

    @pl.when(f == 0)
    def _():
        acc_ref[...] = jnp.zeros_like(acc_ref)

    xb = [x_ref[g, :].astype(BF16) for g in grp]
    h = [jnp.dot(xb[r], wu_ref[...], preferred_element_type=F32) for r in range(ngrp)]
    h = [jnp.square(jnp.maximum(h[r], 0.0)).astype(BF16) for r in range(ngrp)]
    part = [jnp.dot(h[r], wd_ref[...], preferred_element_type=F32) for r in range(ngrp)]
    for r in range(ngrp):
        acc_ref[grp[r], :] += part[r]

    @pl.when(f == pl.num_programs(1) - 1)
    def _():
        o_ref[...] = _layer_norm(alpha * x_ref[...] + acc_ref[...], g_ref[...], b_ref[...])


def _mlp_ln(x, w_up, w_down, g, b, alpha, tm, tf):
    m, d = x.shape
    ff = w_up.shape[1]
    return pl.pallas_call(
        functools.partial(_mlp_kernel, alpha),
        grid=(m // tm, ff // tf),
        in_specs=[pl.BlockSpec((tm, d), lambda i, f: (i, 0)),
                  pl.BlockSpec((d, tf), lambda i, f: (0, f)),
                  pl.BlockSpec((tf, d), lambda i, f: (f, 0)),
                  pl.BlockSpec((1, d), lambda i, f: (0, 0)),
                  pl.BlockSpec((1, d), lambda i, f: (0, 0))],
        out_specs=pl.BlockSpec((tm, d), lambda i, f: (i, 0)),
        out_shape=jax.ShapeDtypeStruct((m, d), F32),
        scratch_shapes=[pltpu.VMEM((tm, d), F32)],
        compiler_params=_params("parallel", "arbitrary"),
        name="mlp_ln",
    )(x, w_up, w_down, g, b)


def _kvq_kernel(x_ref, w_ref, k_ref, v_ref, q_ref):
    e = k_ref.shape[1]
    xb = x_ref[...].astype(BF16)
    k_ref[...] = jnp.dot(xb, w_ref[:, 0:e], preferred_element_type=F32)
    v_ref[...] = jnp.dot(xb, w_ref[:, e:2 * e], preferred_element_type=F32)
    q_ref[...] = jnp.dot(xb, w_ref[:, 2 * e:3 * e], preferred_element_type=F32)


def _kvq(x, w_kvq, tm):
    m, d = x.shape
    e = w_kvq.shape[1] // 3
    row = lambda i: (i, 0)
    return pl.pallas_call(
        _kvq_kernel,
        grid=(m // tm,),
        in_specs=[pl.BlockSpec((tm, d), row), pl.BlockSpec((d, 3 * e), lambda i: (0, 0))],
        out_specs=[pl.BlockSpec((tm, e), row)] * 3,
        out_shape=[jax.ShapeDtypeStruct((m, e), F32)] * 3,
        compiler_params=_params("parallel"),
        name="kvq_proj",
    )(x, w_kvq)


def _kvq_prompt_kernel(x_ref, wk_ref, wvt_ref, wqt_ref, k_ref, kt_ref, vt_ref, qt_ref, km_ref):
    xb = x_ref[...].astype(BF16)
    k = jnp.dot(xb, wk_ref[...], preferred_element_type=F32)
    k_ref[...] = k
    kt_ref[0] = k.T
    vt_ref[0] = _bdot_nt(wvt_ref[...], xb)
    qt_ref[0] = _bdot_nt(wqt_ref[...], xb)
    km_ref[0] = jnp.sum(k, axis=0, keepdims=True) * (1.0 / k.shape[0])


def _kvq_prompt(x, w_k, w_vt, w_qt, b, l):
    m, d = x.shape
    e = w_k.shape[1]
    nb = l // MOBA_BLOCK
    fixed = lambda i, j: (0, 0)
    chan = pl.BlockSpec((1, e, MOBA_BLOCK), lambda i, j: (i, 0, j))
    chan_shape = jax.ShapeDtypeStruct((b, e, l), F32)
    return pl.pallas_call(
        _kvq_prompt_kernel,
        grid=(b, nb),
        in_specs=[pl.BlockSpec((MOBA_BLOCK, d), lambda i, j: (i * nb + j, 0)),
                  pl.BlockSpec((d, e), fixed), pl.BlockSpec((e, d), fixed),
                  pl.BlockSpec((e, d), fixed)],
        out_specs=[pl.BlockSpec((MOBA_BLOCK, e), lambda i, j: (i * nb + j, 0)), chan, chan, chan,
                   pl.BlockSpec((1, 1, e), lambda i, j: (i * nb + j, 0, 0))],
        out_shape=[jax.ShapeDtypeStruct((m, e), F32), chan_shape, chan_shape, chan_shape,
                   jax.ShapeDtypeStruct((m // MOBA_BLOCK, 1, e), F32)],
        compiler_params=_params("parallel", "parallel"),
        name="kvq_prompt",
    )(x, w_k, w_vt, w_qt)


def _gate_terms(ab, a_log, dt_bias):
    return -jnp.exp(a_log) * _softplus(ab + dt_bias)


def _gdn_in_kernel(x_ref, w_ref, wab_ref, wabt_ref, wc_ref, alog_r, dtb_r, alog_c, dtb_c,
                   q_ref, k_ref, v_ref, z_ref, gcol_ref, grow_ref, cst_ref,
                   ext_scr):
    t = pl.program_id(1)
    tm = x_ref.shape[1]
    hw = q_ref.shape[1]
    kw = wc_ref.shape[0]
    dk = hw // GDN_HEADS
    xb = x_ref[0].astype(BF16)

    @pl.when(t == 0)
    def _():
        ext_scr[:, 0:SUBLANES, :] = jnp.zeros((3, SUBLANES, hw), F32)

    outs = (q_ref, k_ref, v_ref)
    bw = 2 * dk
    for s in range(3):
        for c in range(hw // bw):
            cols = slice(s * hw + c * bw, s * hw + (c + 1) * bw)
            blk = slice(c * bw, (c + 1) * bw)
            pre = jnp.dot(xb, w_ref[:, cols], preferred_element_type=F32)
            ext_scr[s, SUBLANES:SUBLANES + tm, blk] = pre
            conv = pre * wc_ref[kw - 1:kw, cols]
            for sh in range(1, kw):
                conv = conv + (ext_scr[s, pl.ds(SUBLANES - sh, tm), blk]
                               * wc_ref[kw - 1 - sh:kw - sh, cols])
            act = _silu(conv)
            if s == 2:
                outs[s][:, blk] = act
            else:
                scale = dk ** -0.5 if s == 0 else 1.0
                for h in range(bw // dk):
                    sl = act[:, h * dk:(h + 1) * dk]
                    ss = jnp.sum(sl * sl, -1, keepdims=True)
                    outs[s][:, c * bw + h * dk:c * bw + (h + 1) * dk] = (
                        sl * (lax.rsqrt(ss + RMS_EPS) * scale))

    z_ref[...] = jnp.dot(xb, w_ref[:, 3 * hw:4 * hw], preferred_element_type=F32)

    ab = jnp.dot(xb, wab_ref[...], preferred_element_type=F32)
    lane = lax.broadcasted_iota(jnp.int32, ab.shape, 1)
    g = jnp.where(lane < GDN_HEADS, _gate_terms(ab, alog_r[...], dtb_r[...]), 0.0)
    gcol_ref[...] = jnp.where(lane < GDN_HEADS, _chunk_cumsum(g, 0), jax.nn.sigmoid(ab))

    abt = _bdot_nt(wabt_ref[...], xb)
    gct = _chunk_cumsum(_gate_terms(abt, alog_c[...], dtb_c[...]), 1)
    for c in range(tm // GDN_CHUNK):
        grow_ref[0, c] = gct[0:GDN_HEADS, c * GDN_CHUNK:(c + 1) * GDN_CHUNK]

    @pl.when(t == pl.num_programs(1) - 1)
    def _():
        for s in range(3):
            cst_ref[0, :, s * hw:(s + 1) * hw] = ext_scr[
                s, pl.ds(SUBLANES + tm - (kw - 1), kw - 1), :]

    for s in range(3):
        ext_scr[s, 0:SUBLANES, :] = ext_scr[s, tm:tm + SUBLANES, :]


def _gdn_in(x, w_qkvz, w_ab, w_abt, w_conv, alog_r, dtb_r, alog_c, dtb_c, tm):
    b, l, d = x.shape
    hw = w_qkvz.shape[1] // 4
    kw = w_conv.shape[0]
    nt = l // tm
    cpt = tm // GDN_CHUNK
    row = lambda i, t: (i * nt + t, 0)
    fixed = lambda i, t: (0, 0)
    act = jax.ShapeDtypeStruct((b * l, hw), F32)
    return pl.pallas_call(
        _gdn_in_kernel,
        grid=(b, nt),
        in_specs=[pl.BlockSpec((1, tm, d), lambda i, t: (i, t, 0)),
                  pl.BlockSpec(w_qkvz.shape, fixed), pl.BlockSpec(w_ab.shape, fixed),
                  pl.BlockSpec(w_abt.shape, fixed), pl.BlockSpec(w_conv.shape, fixed),
                  pl.BlockSpec(alog_r.shape, fixed), pl.BlockSpec(dtb_r.shape, fixed),
                  pl.BlockSpec(alog_c.shape, fixed), pl.BlockSpec(dtb_c.shape, fixed)],
        out_specs=[pl.BlockSpec((tm, hw), row)] * 4 + [
            pl.BlockSpec((tm, LANES), row),
            pl.BlockSpec((1, cpt, GDN_HEADS, GDN_CHUNK), lambda i, t: (i, t, 0, 0)),
            pl.BlockSpec((1, kw - 1, 3 * hw), lambda i, t: (i, 0, 0))],
        out_shape=[act] * 4 + [
            jax.ShapeDtypeStruct((b * l, LANES), F32),
            jax.ShapeDtypeStruct((b, l // GDN_CHUNK, GDN_HEADS, GDN_CHUNK), F32),
            jax.ShapeDtypeStruct((b, kw - 1, 3 * hw), F32)],
        scratch_shapes=[pltpu.VMEM((3, SUBLANES + tm, hw), F32)],
        compiler_params=_params("parallel", "arbitrary"),
        name="gdn_in",
    )(x, w_qkvz, w_ab, w_abt, w_conv, alog_r, dtb_r, alog_c, dtb_c)


def _gdn_chunk_kernel(q_ref, k_ref, v_ref, z_ref, gcol_ref, grow_ref, wn_ref,
                      o_ref, sfin_ref, s_scr):
    c = pl.program_id(1)
    cs = GDN_CHUNK
    dk = q_ref.shape[1] // GDN_HEADS

    @pl.when(c == 0)
    def _():
        s_scr[...] = jnp.zeros_like(s_scr)

    ri = lax.broadcasted_iota(jnp.int32, (cs, cs), 0)
    ci = lax.broadcasted_iota(jnp.int32, (cs, cs), 1)
    incl = ri >= ci
    strict = ri > ci
    eye = (ri == ci).astype(F32)
    heads = range(GDN_HEADS)
    cpb = q_ref.shape[0] // cs
    units = [(j, h) for j in range(cpb) for h in heads]
    rows = {j: slice(j * cs, (j + 1) * cs) for j in range(cpb)}
    cols = {h: slice(h * dk, (h + 1) * dk) for h in heads}
    each = lambda f: {unit: f(*unit) for unit in units}
    gcol = {j: gcol_ref[rows[j], :] for j in range(cpb)}
    grow = {j: grow_ref[0, j] for j in range(cpb)}
    q = each(lambda j, h: q_ref[rows[j], cols[h]])
    k = each(lambda j, h: k_ref[rows[j], cols[h]])
    qb = each(lambda j, h: q[j, h].astype(BF16))
    kbf = each(lambda j, h: k[j, h].astype(BF16))
    gcc = each(lambda j, h: gcol[j][:, h:h + 1])
    beta = each(lambda j, h: gcol[j][:, GDN_HEADS + h:GDN_HEADS + h + 1])
    decay = each(lambda j, h: jnp.exp(jnp.where(incl, gcc[j, h] - grow[j][h:h + 1, :], NEG)))
    eg = each(lambda j, h: jnp.exp(gcc[j, h]))
    glast = each(lambda j, h: gcc[j, h][cs - 1:cs, :])
    kb = each(lambda j, h: k[j, h] * beta[j, h])
    kk = each(lambda j, h: _bdot_nt(kb[j, h], kbf[j, h]))
    qk = each(lambda j, h: _bdot_nt(qb[j, h], kbf[j, h]))
    p = each(lambda j, h: jnp.where(strict, -kk[j, h] * decay[j, h], 0.0))
    tinv = each(lambda j, h: eye + p[j, h])
    for _ in range(int(math.log2(cs)) - 1):
        pb = each(lambda j, h: p[j, h].astype(BF16))
        p = each(lambda j, h: _bdot(pb[j, h], pb[j, h]))
        tinv = each(lambda j, h: tinv[j, h] + _bdot(tinv[j, h], p[j, h]))
    tb = each(lambda j, h: tinv[j, h].astype(BF16))
    u = each(lambda j, h: _bdot(tb[j, h], v_ref[rows[j], cols[h]] * beta[j, h]))
    w = each(lambda j, h: _bdot(tb[j, h], kb[j, h] * eg[j, h]).astype(BF16))
    qg = each(lambda j, h: (q[j, h] * eg[j, h]).astype(BF16))
    attn = each(lambda j, h: (qk[j, h] * decay[j, h]).astype(BF16))
    ktail = each(lambda j, h: (k[j, h] * jnp.exp(glast[j, h] - gcc[j, h])).astype(BF16))
    s = [s_scr[h] for h in heads]
    for j in range(cpb):
        sb = [s[h].astype(BF16) for h in heads]
        v_new = [(u[j, h] - _bdot(w[j, h], sb[h])).astype(BF16) for h in heads]
        o = [_bdot(qg[j, h], sb[h]) + _bdot(attn[j, h], v_new[h]) for h in heads]
        s = [s[h] * jnp.exp(glast[j, h]) + _bdot_tn(ktail[j, h], v_new[h]) for h in heads]
        for h in heads:
            on = o[h] * lax.rsqrt(jnp.mean(o[h] * o[h], -1, keepdims=True) + RMS_EPS) * wn_ref[...]
            o_ref[rows[j], cols[h]] = (on * _silu(z_ref[rows[j], cols[h]])).astype(o_ref.dtype)
    for h in heads:
        s_scr[h] = s[h]

    @pl.when(c == pl.num_programs(1) - 1)
    def _():
        sfin_ref[0] = s_scr[...]


def _gdn_chunks(q, k, v, z, gcol, grow, w_onorm, b, l, cpb):
    hw = q.shape[1]
    dk = hw // GDN_HEADS
    nc = l // (cpb * GDN_CHUNK)
    tr = cpb * GDN_CHUNK
    row = lambda i, c: (i * nc + c, 0)
    return pl.pallas_call(
        _gdn_chunk_kernel,
        grid=(b, nc),
        in_specs=[pl.BlockSpec((tr, hw), row)] * 4 + [
            pl.BlockSpec((tr, LANES), row),
            pl.BlockSpec((1, cpb, GDN_HEADS, GDN_CHUNK), lambda i, c: (i, c, 0, 0)),
            pl.BlockSpec((1, dk), lambda i, c: (0, 0))],
        out_specs=[pl.BlockSpec((tr, hw), row),
                   pl.BlockSpec((1, GDN_HEADS, dk, dk), lambda i, c: (i, 0, 0, 0))],
        out_shape=[jax.ShapeDtypeStruct((b * l, hw), BF16),
                   jax.ShapeDtypeStruct((b, GDN_HEADS, dk, dk), F32)],
        scratch_shapes=[pltpu.VMEM((GDN_HEADS, dk, dk), F32)],
        compiler_params=_params("parallel", "arbitrary"),
        name="gdn_chunks",
    )(q, k, v, z, gcol, grow, w_onorm)


def _gdn_in_sample_kernel(x_ref, w_ref, wab_ref, wc_ref, alog_r, dtb_r, sc_ref,
                          q_ref, k_ref, v_ref, z_ref, gate_ref, cst_ref):
    hw = q_ref.shape[1]
    kw = wc_ref.shape[0]
    dk = hw // GDN_HEADS
    xb = x_ref[...].astype(BF16)
    outs = (q_ref, k_ref, v_ref)
    for s in range(3):
        cols = slice(s * hw, (s + 1) * hw)
        pre = jnp.dot(xb, w_ref[:, cols], preferred_element_type=F32)
        conv = pre * wc_ref[kw - 1:kw, cols]
        for j in range(kw - 1):
            conv = conv + sc_ref[j, :, cols] * wc_ref[j:j + 1, cols]
        cst_ref[kw - 2, :, cols] = pre
        act = _silu(conv)
        if s == 2:
            outs[s][...] = act
        else:
            scale = dk ** -0.5 if s == 0 else 1.0
            for h in range(GDN_HEADS):
                hs = slice(h * dk, (h + 1) * dk)
                sl = act[:, hs]
                ss = jnp.sum(sl * sl, -1, keepdims=True)
                outs[s][:, hs] = sl * (lax.rsqrt(ss + RMS_EPS) * scale)
    cst_ref[0:kw - 2] = sc_ref[1:kw - 1]
    z_ref[...] = jnp.dot(xb, w_ref[:, 3 * hw:4 * hw], preferred_element_type=F32)
    ab = jnp.dot(xb, wab_ref[...], preferred_element_type=F32)
    lane = lax.broadcasted_iota(jnp.int32, ab.shape, 1)
    gate_ref[...] = jnp.where(lane < GDN_HEADS, _gate_terms(ab, alog_r[...], dtb_r[...]),
                              jax.nn.sigmoid(ab))


def _gdn_in_sample(x, w_qkvz, w_ab, w_conv, alog_r, dtb_r, conv_state):
    m = x.shape[0]
    hw = w_qkvz.shape[1] // 4
    act = jax.ShapeDtypeStruct((m, hw), F32)
    return pl.pallas_call(
        _gdn_in_sample_kernel,
        out_shape=[act] * 4 + [jax.ShapeDtypeStruct((m, LANES), F32),
                               jax.ShapeDtypeStruct(conv_state.shape, F32)],
        compiler_params=pltpu.CompilerParams(vmem_limit_bytes=VMEM_LIMIT_BYTES),
        name="gdn_in_sample",
    )(x, w_qkvz, w_ab, w_conv, alog_r, dtb_r, conv_state)


def _gdn_step_kernel(q_ref, k_ref, v_ref, z_ref, gate_ref, wn_ref, s_ref, o_ref, snew_ref):
    dk = s_ref.shape[2]
    gate = gate_ref[0]
    for h in range(GDN_HEADS):
        hs = slice(h * dk, (h + 1) * dk)
        q = q_ref[0, :, hs]
        k = k_ref[0, :, hs]
        v = v_ref[0, :, hs]
        eg = jnp.exp(gate[:, h:h + 1])
        beta = gate[:, GDN_HEADS + h:GDN_HEADS + h + 1]
        s = s_ref[0, h]
        kc = jnp.broadcast_to(k, (dk, dk)).T
        qc = jnp.broadcast_to(q, (dk, dk)).T
        ks = jnp.sum(kc * s, axis=0, keepdims=True)
        qs = jnp.sum(qc * s, axis=0, keepdims=True)
        v_new = beta * (v - eg * ks)
        o = eg * qs + jnp.sum(q * k, -1, keepdims=True) * v_new
        snew_ref[0, h] = eg * s + kc * v_new
        o = o * lax.rsqrt(jnp.mean(o * o, -1, keepdims=True) + RMS_EPS) * wn_ref[...]
        o_ref[0, :, hs] = o * _silu(z_ref[0, :, hs])


def _gdn_step(q, k, v, z, gate, w_onorm, state):
    m, hw = q.shape
    dk = hw // GDN_HEADS
    r3 = lambda a: a.reshape(m, 1, a.shape[1])
    tok = pl.BlockSpec((1, 1, hw), lambda i: (i, 0, 0))
    st = pl.BlockSpec((1, GDN_HEADS, dk, dk), lambda i: (i, 0, 0, 0))
    o, s_new = pl.pallas_call(
        _gdn_step_kernel,
        grid=(m,),
        in_specs=[tok] * 4 + [pl.BlockSpec((1, 1, LANES), lambda i: (i, 0, 0)),
                              pl.BlockSpec((1, dk), lambda i: (0, 0)), st],
        out_specs=[tok, st],
        out_shape=[jax.ShapeDtypeStruct((m, 1, hw), F32),
                   jax.ShapeDtypeStruct(state.shape, F32)],
        compiler_params=_params("parallel"),
        name="gdn_step",
    )(r3(q), r3(k), r3(v), r3(z), r3(gate), w_onorm, state)
    return o.reshape(m, hw), s_new


def _zigzag(step, n):
    return jnp.where(step % 2 == 0, step // 2, n - 1 - step // 2)


def _select_init(t, q_ref, qcol_scr, gate_scr):
    @pl.when(t == 0)
    def _():
        e, page = qcol_scr.shape
        qcol_scr[...] = jnp.broadcast_to(q_ref[0], (page, e)).T
        gate_scr[...] = jnp.zeros_like(gate_scr)


def _select_fold(ppb, t, pages, qcol_scr, gate_scr):
    e, page = qcol_scr.shape
    hd = e // MOBA_HEADS
    bps = len(pages) // ppb
    lane = lax.broadcasted_iota(jnp.int32, (MOBA_HEADS, LANES), 1)
    gate = gate_scr[...]
    qcol = qcol_scr[...]
    for blk in range(bps):
        tot = pages[blk * ppb][0]
        for j in range(1, ppb):
            tot = tot + pages[blk * ppb + j][0]
        per_head = jnp.sum((tot * qcol).reshape(MOBA_HEADS, hd, page), axis=1)
        g = jnp.sum(per_head, -1, keepdims=True) * (1.0 / MOBA_BLOCK)
        gate = jnp.where(lane == t * bps + blk, g, gate)
    gate_scr[...] = gate


def _select_pick(n_sel, nbf, last, idx_ref, gate_scr):
    @pl.when(last)
    def _():
        lane = lax.broadcasted_iota(jnp.int32, (MOBA_HEADS, LANES), 1)
        lane_f = lane.astype(F32)
        g = jnp.where(lane < nbf, gate_scr[...], -jnp.inf)
        picks = jnp.zeros((MOBA_HEADS, LANES), F32)
        for r in range(n_sel):
            best = jnp.max(g, -1, keepdims=True)
            idx = jnp.min(jnp.where(g == best, lane_f, float(LANES)), -1, keepdims=True)
            picks = jnp.where(lane == r, idx, picks)
            g = jnp.where(lane_f == idx, -jnp.inf, g)
        idx_ref[0] = picks.astype(jnp.int32)


def _moba_prompt_kernel(n_sel, n_sel_s, ppb, nbf, pt_ref, slopes_ref, qt_ref, k_ref, vt_ref,
                        km_ref, qs_ref, *rest):
    del pt_ref
    pages = rest[:-4]
    o_ref, idx_ref, qcol_scr, gate_scr = rest[-4:]
    p = pl.program_id(1)
    i = _zigzag(pl.program_id(2), pl.num_programs(2))
    step = (pl.program_id(0) * pl.num_programs(1) + p) * pl.num_programs(2) + pl.program_id(2)
    sps = nbf * ppb // len(pages)
    t = step % sps
    _select_init(t, qs_ref, qcol_scr, gate_scr)
    bs = MOBA_BLOCK
    hd = LANES // 2
    nb = km_ref.shape[1]
    nbp = -(-nb // SUBLANES) * SUBLANES
    rel = (lax.broadcasted_iota(jnp.int32, (bs, bs), 1)
           - lax.broadcasted_iota(jnp.int32, (bs, bs), 0))
    aug_lane = lax.broadcasted_iota(jnp.int32, (bs, LANES), 1)
    k_idx = lax.broadcasted_iota(jnp.int32, (bs, LANES), 0).astype(F32)
    k_aug = jnp.where(aug_lane < 3, k_idx, jnp.where(aug_lane < 6, 1.0, 0.0)).astype(BF16)

    def query_side():
        qt = qt_ref[0]
        km = km_ref[0]
        if nbp > nb:
            km = jnp.concatenate([km, jnp.zeros((nbp - nb, LANES), F32)], axis=0)
        k_hi, k_mid, _ = _split3(km)
        chan = lax.broadcasted_iota(jnp.int32, (LANES, bs), 0)
        blk = lax.broadcasted_iota(jnp.int32, (nbp, bs), 0)
        q_idx = lax.broadcasted_iota(jnp.int32, (1, bs), 1).astype(F32)
        aug_row = lax.broadcasted_iota(jnp.int32, (2 * SUBLANES, bs), 0)
        wqs, keeps, slope_h = [], [], []
        for hh in range(2):
            qh = jnp.where((chan // hd) == hh, qt, 0.0)
            q_hi, q_mid, _ = _split3(qh)
            gate = _bdot(k_hi, q_hi) + _bdot(k_hi, q_mid) + _bdot(k_mid, q_hi)
            cnt = jnp.zeros((nbp, bs), jnp.int32)
            for m in range(nb):
                gm = gate[m:m + 1, :]
                beats = (gm > gate) | ((gm == gate) & (m < blk))
                cnt = cnt + jnp.where(beats, jnp.where(m < i, 1, 0), 0)
            keeps.append((blk < i) & (cnt < n_sel))
            slope2 = slopes_ref[2 * p + hh] * LOG2E
            slope_h.append(slope2)
            s_parts = _split3(jnp.full((1, bs), slope2, F32))
            t_parts = _split3(-slope2 * q_idx)
            aug = jnp.zeros((2 * SUBLANES, bs), F32)
            for r, part in enumerate(s_parts + t_parts):
                aug = jnp.where(aug_row == r, part.astype(F32), aug)
            wqs.append(jnp.concatenate(
                [(qh * (hd ** -0.5 * LOG2E)).astype(BF16), aug.astype(BF16),
                 jnp.zeros((LANES - 2 * SUBLANES, bs), BF16)], axis=0))
        return wqs, keeps, slope_h

    def attend(n):
        _select_fold(ppb, t, pages, qcol_scr, gate_scr)
        wqs, keeps, slope_h = query_side()
        pair = range(2)
        kcat = jnp.concatenate([k_ref[0:n * bs, :].astype(BF16),
                                jnp.concatenate([k_aug] * n, axis=0)], axis=1)
        s_all = [jnp.dot(kcat, wqs[hh], preferred_element_type=F32) for hh in pair]
        us, shifts = [], []
        for hh in pair:
            u = [s_all[hh][j * bs:(j + 1) * bs, :] for j in range(n - 1)]
            u.append(jnp.where(rel >= 0, s_all[hh][(n - 1) * bs:n * bs, :], NEG))
            rows = [jnp.where(keeps[hh][j:j + 1, :], -slope_h[hh] * float((n - 1 - j) * bs), NEG)
                    for j in range(n - 1)] + [jnp.zeros((1, bs), F32)]
            mx = functools.reduce(
                jnp.maximum, [jnp.max(u[j], 0, keepdims=True) + rows[j] for j in range(n)])
            us.append(u)
            shifts.append([mx - rows[j] for j in range(n)])
        ps = [[jnp.exp2(us[hh][j] - shifts[hh][j]) for j in range(n)] for hh in pair]
        den = [sum(jnp.sum(pm, 0, keepdims=True) for pm in ps[hh]) for hh in pair]
        pcat = [jnp.concatenate([pm.astype(BF16) for pm in ps[hh]], axis=0) for hh in pair]
        acc = [_bdot(vt_ref[0, hh * hd:(hh + 1) * hd, 0:n * bs], pcat[hh]) for hh in pair]
        o_ref[...] = jnp.concatenate([acc[hh] / den[hh] for hh in pair],
                                     axis=0).T.astype(o_ref.dtype)

    for n in range(1, nb + 1):
        pl.when(i == n - 1)(functools.partial(attend, n))
    _select_pick(n_sel_s, nbf, t == sps - 1, idx_ref, gate_scr)


def _moba_prompt(qt, k, vt, k_mean, slopes, b, l, q_s, cache_kt, page_table, n_sel_s):
    e = k.shape[1]
    nb = l // MOBA_BLOCK
    n_sel = min(MOBA_TOPK, (l - 1) // MOBA_BLOCK)
    npair = e // LANES
    m = q_s.shape[0]
    _, _, page = cache_kt.shape
    n_pages = page_table.shape[1]
    ppb = MOBA_BLOCK // page
    nbf = n_pages // ppb
    steps = b * npair * nb
    assert steps % m == 0 and nbf <= LANES and page == LANES
    sps = steps // m
    assert n_pages % sps == 0 and (n_pages // sps) % ppb == 0
    pps = n_pages // sps
    seq = lambda bi, p, i: ((bi * npair + p) * nb + i) // sps
    tstep = lambda bi, p, i: ((bi * npair + p) * nb + i) % sps

    def page_spec(j):
        return pl.BlockSpec(
            (1, e, page), lambda bi, p, i, pt: (pt[seq(bi, p, i), tstep(bi, p, i) * pps + j], 0, 0))

    return pl.pallas_call(
        functools.partial(_moba_prompt_kernel, n_sel, n_sel_s, ppb, nbf),
        grid_spec=pltpu.PrefetchScalarGridSpec(
            num_scalar_prefetch=1,
            grid=(b, npair, nb),
            in_specs=[pl.BlockSpec(memory_space=pltpu.SMEM),
                      pl.BlockSpec((1, LANES, MOBA_BLOCK),
                                   lambda bi, p, i, pt: (bi, p, _zigzag(i, nb))),
                      pl.BlockSpec((l, LANES), lambda bi, p, i, pt: (bi, p)),
                      pl.BlockSpec((1, LANES, l), lambda bi, p, i, pt: (bi, p, 0)),
                      pl.BlockSpec((1, nb, LANES), lambda bi, p, i, pt: (bi, 0, p)),
                      pl.BlockSpec((1, 1, e), lambda bi, p, i, pt: (seq(bi, p, i), 0, 0))]
            + [page_spec(j) for j in range(pps)],
            out_specs=[pl.BlockSpec((MOBA_BLOCK, LANES),
                                    lambda bi, p, i, pt: (bi * nb + _zigzag(i, nb), p)),
                       pl.BlockSpec((1, MOBA_HEADS, LANES),
                                    lambda bi, p, i, pt: (seq(bi, p, i), 0, 0))],
            scratch_shapes=[pltpu.VMEM((e, page), F32), pltpu.VMEM((MOBA_HEADS, LANES), F32)]),
        out_shape=[jax.ShapeDtypeStruct((b * l, e), BF16),
                   jax.ShapeDtypeStruct((m, MOBA_HEADS, LANES), jnp.int32)],
        compiler_params=_params("arbitrary", "arbitrary", "arbitrary"),
        name="moba_prompt",
    )(page_table, slopes, qt, k, vt, k_mean, q_s.reshape(m, 1, e), *([cache_kt] * pps))


def _moba_sample_kernel(n_sel, ppb, past, hps, pt_ref, sel_ref, slopes_ref, q_ref, kn_ref, vn_ref,
                        *refs):
    del pt_ref
    tph = n_sel * ppb
    k_tiles = refs[:hps * tph]
    v_tiles = refs[hps * tph:2 * hps * tph]
    o_ref = refs[2 * hps * tph]
    bi = pl.program_id(0)
    grp = pl.program_id(1)
    hd, page = k_tiles[0].shape[1:]
    pos_in_page = lax.broadcasted_iota(jnp.int32, (SUBLANES, page), 1)
    q = q_ref[0]
    kn = kn_ref[0]
    vn = vn_ref[0]
    heads = range(hps)
    tiles = range(tph)
    ls = [slice(hh * hd, (hh + 1) * hd) for hh in heads]
    qh = [q[:, ls[hh]] * hd ** -0.5 for hh in heads]
    q8 = [jnp.broadcast_to(qh[hh], (SUBLANES, hd)).astype(BF16) for hh in heads]
    s_own = [jnp.sum(qh[hh] * kn[:, ls[hh]], -1, keepdims=True) for hh in heads]
    raw = [[jnp.dot(q8[hh], k_tiles[hh * tph + n][0].astype(BF16), preferred_element_type=F32)
            for n in tiles] for hh in heads]
    scores = []
    for hh in heads:
        h = grp * hps + hh
        slope = slopes_ref[h]
        row = []
        for n in tiles:
            blk = sel_ref[bi, h * n_sel + n // ppb]
            dist = (past - blk * MOBA_BLOCK - (n % ppb) * page - pos_in_page).astype(F32)
            row.append(raw[hh][n] - slope * dist)
        scores.append(row)
    mx = [functools.reduce(jnp.maximum,
                           [jnp.max(sc, -1, keepdims=True)[0:1] for sc in scores[hh]], s_own[hh])
          for hh in heads]
    pm = [[jnp.exp(sc - mx[hh]) for sc in scores[hh]] for hh in heads]
    pv = [[_bdot_nt(pm[hh][n], v_tiles[hh * tph + n][0])[0:1] for n in tiles] for hh in heads]
    outs = []
    for hh in heads:
        p_own = jnp.exp(s_own[hh] - mx[hh])
        den = p_own + sum(jnp.sum(pm[hh][n], -1, keepdims=True)[0:1] for n in tiles)
        acc = p_own * vn[:, ls[hh]] + sum(pv[hh])
        outs.append(acc / den)
    o_ref[0] = jnp.concatenate(outs, axis=-1)


def _moba_sample(q, k_new, v_new, cache_kt, cache_vt, page_table, sel, slopes, n_sel):
    m, e = q.shape
    _, _, page = cache_kt.shape
    hd = e // MOBA_HEADS
    ppb = MOBA_BLOCK // page
    past = page_table.shape[1] * page
    hps = 8
    assert MOBA_HEADS % hps == 0 and (hps * hd) % LANES == 0
    r3 = lambda a: a.reshape(m, 1, e)
    tok = pl.BlockSpec((1, 1, hps * hd), lambda bi, g, pt, sl: (bi, 0, g))

    def tile_spec(hh, s, j):
        def index(bi, g, pt, sl):
            h = g * hps + hh
            return pt[bi, sl[bi, h * n_sel + s] * ppb + j], h, 0
        return pl.BlockSpec((1, hd, page), index)

    tiles = [tile_spec(hh, s, j) for hh in range(hps) for s in range(n_sel) for j in range(ppb)]
    out = pl.pallas_call(
        functools.partial(_moba_sample_kernel, n_sel, ppb, past, hps),
        grid_spec=pltpu.PrefetchScalarGridSpec(
            num_scalar_prefetch=2,
            grid=(m, MOBA_HEADS // hps),
            in_specs=[pl.BlockSpec(memory_space=pltpu.SMEM), tok, tok, tok] + tiles + tiles,
            out_specs=tok),
        out_shape=jax.ShapeDtypeStruct((m, 1, e), F32),
        compiler_params=_params("parallel", "parallel"),
        name="moba_sample",
    )(page_table, sel, slopes, r3(q), r3(k_new), r3(v_new),
      *([cache_kt] * len(tiles)), *([cache_vt] * len(tiles)))
    return out.reshape(m, e)


def _row_tile(m, cap):
    t = min(m, cap)
    assert m % t == 0
    return t


def kernel(x_prompt, x_sample, cache_k, cache_v, state_delta, state_conv, page_table,
           gdn_w_in, gdn_w_conv, gdn_a_log, gdn_dt_bias, gdn_w_onorm, gdn_w_o,
           w_kv, moba_w_q, moba_w_o, mlp_w_up, mlp_w_down, ln_g, ln_b):
    bp, lp, d = x_prompt.shape
    bs, ls, _ = x_sample.shape
    depth = mlp_w_up.shape[0]
    n_a = gdn_w_in.shape[0]
    alpha = (2 * depth) ** 0.25
    n_pool, page = cache_k.shape[0], cache_k.shape[1]
    e_kv = cache_k.shape[2] * cache_k.shape[3]
    hw = gdn_w_o.shape[1]
    conv_ch = gdn_w_conv.shape[2]
    kw = gdn_w_conv.shape[1]
    n_pages = page_table.shape[1]
    assert ls == 1, "sample stream handles one new token per sequence"
    assert conv_ch == 3 * hw and kw - 1 <= SUBLANES and kw >= 3
    assert lp % MOBA_BLOCK == 0 and MOBA_BLOCK % GDN_CHUNK == 0 and MOBA_BLOCK % page == 0
    assert (n_pages * page) % MOBA_BLOCK == 0, "past length must end on a MoBA block boundary"
    assert e_kv == MOBA_HEADS * (LANES // 2)
    assert depth - n_a == 1, "one MoBA layer reads the shared K/V"
    nbf = n_pages * page // MOBA_BLOCK
    n_sel_s = min(MOBA_TOPK, nbf)
    assert n_sel_s >= 1

    xp = x_prompt.reshape(bp * lp, d)
    xs = x_sample.reshape(bs, d)
    tm_p = _row_tile(bp * lp, 512)
    tm_s = _row_tile(bs, 512)
    row = lambda a: a.reshape(1, -1).astype(F32)
    heads = jnp.arange(1, MOBA_HEADS + 1, dtype=F32)
    slopes = jnp.exp2(-ALIBI_MAX_EXP * heads / MOBA_HEADS)
    cache_kt = cache_k.transpose(0, 2, 3, 1).reshape(n_pool, e_kv, page)
    cache_vt = cache_v.transpose(0, 2, 3, 1).reshape(n_pool, e_kv, page)

    conv_p, delta_p, conv_s, delta_s = [], [], [], []
    for layer in range(depth):
        g0, b0 = row(ln_g[layer, 0]), row(ln_b[layer, 0])
        g1, b1 = row(ln_g[layer, 1]), row(ln_b[layer, 1])
        if layer < n_a:
            w_in = gdn_w_in[layer]
            w_qkvz = w_in[:, :4 * hw].astype(BF16)
            w_gate = w_in[:, 4 * hw:]
            w_ab = jnp.pad(w_gate, ((0, 0), (0, LANES - 2 * GDN_HEADS))).astype(BF16)
            w_abt = w_gate.T.astype(BF16)
            pad_r = lambda a: jnp.pad(row(a), ((0, 0), (0, LANES - GDN_HEADS)))
            pad_c = lambda a: jnp.pad(a.astype(F32).reshape(-1, 1), ((0, GDN_HEADS), (0, 0)))
            alog_r, dtb_r = pad_r(gdn_a_log[layer]), pad_r(gdn_dt_bias[layer])
            alog_c, dtb_c = pad_c(gdn_a_log[layer]), pad_c(gdn_dt_bias[layer])
            w_conv = gdn_w_conv[layer].astype(F32)
            w_onorm = row(gdn_w_onorm[layer])
            w_o = gdn_w_o[layer].astype(BF16)

            q, k, v, z, gcol, grow, cst = _gdn_in(
                x_prompt if layer == 0 else xp.reshape(bp, lp, d),
                w_qkvz, w_ab, w_abt, w_conv, alog_r, dtb_r, alog_c, dtb_c, tm=MOBA_BLOCK)
            o, s_fin = _gdn_chunks(q, k, v, z, gcol, grow, w_onorm, bp, lp, cpb=4)
            conv_p.append(cst)
            delta_p.append(s_fin)
            mix_p = (o, w_o)

            q, k, v, z, gate, cst = _gdn_in_sample(
                xs, w_qkvz, w_ab, w_conv, alog_r, dtb_r,
                state_conv[layer].astype(F32).transpose(1, 0, 2))
            o, s_new = _gdn_step(q, k, v, z, gate, w_onorm, state_delta[layer].astype(F32))
            conv_s.append(cst.transpose(1, 0, 2))
            delta_s.append(s_new)
            mix_s = (o, w_o)
        else:
            w_q = moba_w_q[layer - n_a]
            k_row, kt_p, vt_p, qt_p, km_p = _kvq_prompt(
                xp, w_kv[:, :e_kv].astype(BF16), w_kv[:, e_kv:].T.astype(BF16),
                w_q.T.astype(BF16), bp, lp)
            k_s, v_s, q_s = _kvq(xs, jnp.concatenate([w_kv, w_q], axis=1).astype(BF16), tm_s)
            w_o = moba_w_o[layer - n_a].astype(BF16)
            o, sel = _moba_prompt(qt_p, k_row, vt_p, km_p.reshape(bp, lp // MOBA_BLOCK, e_kv),
                                  slopes, bp, lp, q_s, cache_kt, page_table, n_sel_s)
            mix_p = (o, w_o)
            sel = sel[:, :, :n_sel_s].reshape(bs, MOBA_HEADS * n_sel_s)
            o = _moba_sample(q_s, k_s, v_s, cache_kt, cache_vt, page_table, sel, slopes, n_sel_s)
            mix_s = (o, w_o)
        w_up = mlp_w_up[layer].astype(BF16)
        w_down = mlp_w_down[layer].astype(BF16)
        xp = _proj_ln(mix_p[0], xp, mix_p[1], g0, b0, alpha, tm_p)
        xs = _proj_ln(mix_s[0], xs, mix_s[1], g0, b0, alpha, tm_s)
        xp = _mlp_ln(xp, w_up, w_down, g1, b1, alpha, tm_p, 2048)
        xs = _mlp_ln(xs, w_up, w_down, g1, b1, alpha, tm_s, 1024)

    hd = e_kv // MOBA_HEADS
    tokens_major = lambda a: a.reshape(bp, MOBA_HEADS, hd, lp).transpose(0, 3, 1, 2)
    return (xp.reshape(bp, lp, d), xs.reshape(bs, ls, d),
            tokens_major(kt_p), tokens_major(vt_p),
            jnp.stack(delta_p), jnp.stack(conv_p),
            k_s.reshape(bs, ls, MOBA_HEADS, hd), v_s.reshape(bs, ls, MOBA_HEADS, hd),
            jnp.stack(delta_s), jnp.stack(conv_s))
```

```python
import functools
import math

import jax
import jax.numpy as jnp
from jax import lax
from jax.experimental import pallas as pl
from jax.experimental.pallas import tpu as pltpu

F32 = jnp.float32
BF16 = jnp.bfloat16

GDN_HEADS = 8
GDN_CHUNK = 64
MOBA_HEADS = 16
MOBA_BLOCK = 256
MOBA_TOPK = 3
ALIBI_MAX_EXP = 16
LN_EPS = 1e-5
RMS_EPS = 1e-6

LANES = 128
SUBLANES = 8
VMEM_LIMIT_BYTES = 48 * 1024 * 1024

NEG = -1e30
LOG2E = math.log2(math.e)


def _params(*semantics):
    return pltpu.CompilerParams(dimension_semantics=semantics,
                                vmem_limit_bytes=VMEM_LIMIT_BYTES)


def _bdot(a, b):
    return jnp.dot(a.astype(BF16), b.astype(BF16), preferred_element_type=F32)


def _bdot_nt(a, b):
    return lax.dot_general(a.astype(BF16), b.astype(BF16), (((1,), (1,)), ((), ())),
                           preferred_element_type=F32)


def _bdot_tn(a, b):
    return lax.dot_general(a.astype(BF16), b.astype(BF16), (((0,), (0,)), ((), ())),
                           preferred_element_type=F32)


def _split3(a):
    hi = a.astype(BF16)
    r = a - hi.astype(F32)
    mid = r.astype(BF16)
    lo = (r - mid.astype(F32)).astype(BF16)
    return hi, mid, lo


def _layer_norm(h, g, b):
    mu = jnp.mean(h, -1, keepdims=True)
    d = h - mu
    var = jnp.mean(d * d, -1, keepdims=True)
    return d * lax.rsqrt(var + LN_EPS) * g + b


def _silu(x):
    return x * jax.nn.sigmoid(x)


def _softplus(x):
    return jnp.maximum(x, 0.0) + jnp.log(1.0 + jnp.exp(-jnp.abs(x)))


def _chunk_cumsum(g, axis):
    pos = lax.broadcasted_iota(jnp.int32, g.shape, axis) % GDN_CHUNK
    sh = 1
    while sh < GDN_CHUNK:
        g = g + jnp.where(pos >= sh, pltpu.roll(g, sh, axis), 0.0)
        sh *= 2
    return g


def _proj_ln_kernel(alpha, a_ref, x_ref, w_ref, g_ref, b_ref, o_ref):
    sub = jnp.dot(a_ref[...].astype(BF16), w_ref[...], preferred_element_type=F32)
    o_ref[...] = _layer_norm(alpha * x_ref[...] + sub, g_ref[...], b_ref[...])


def _proj_ln(a, x, w, g, b, alpha, tm):
    m, d = x.shape
    e = a.shape[1]
    row = lambda i: (i, 0)
    fixed = lambda i: (0, 0)
    return pl.pallas_call(
        functools.partial(_proj_ln_kernel, alpha),
        grid=(m // tm,),
        in_specs=[pl.BlockSpec((tm, e), row), pl.BlockSpec((tm, d), row),
                  pl.BlockSpec((e, d), fixed), pl.BlockSpec((1, d), fixed),
                  pl.BlockSpec((1, d), fixed)],
        out_specs=pl.BlockSpec((tm, d), row),
        out_shape=jax.ShapeDtypeStruct((m, d), F32),
        compiler_params=_params("parallel"),
        name="proj_ln",
    )(a, x, w, g, b)


def _mlp_kernel(alpha, x_ref, wu_ref, wd_ref, g_ref, b_ref, o_ref, acc_ref):
    f = pl.program_id(1)
    tm = x_ref.shape[0]
    ngrp = 2 if tm % (2 * 2 * SUBLANES) == 0 else 1
    grp = [slice(r * tm // ngrp, (r + 1) * tm // ngrp) for r in range(ngrp)]

    @pl.when(f == 0)
    def _():
        acc_ref[...] = jnp.zeros_like(acc_ref)

    xb = [x_ref[g, :].astype(BF16) for g in grp]
    h = [jnp.dot(xb[r], wu_ref[...], preferred_element_type=F32) for r in range(ngrp)]
    h = [jnp.square(jnp.maximum(h[r], 0.0)).astype(BF16) for r in range(ngrp)]
    part = [jnp.dot(h[r], wd_ref[...], preferred_element_type=F32) for r in range(ngrp)]
    for r in range(ngrp):
        acc_ref[grp[r], :] += part[r]

    @pl.when(f == pl.num_programs(1) - 1)
    def _():
        o_ref[...] = _layer_norm(alpha * x_ref[...] + acc_ref[...], g_ref[...], b_ref[...])


def _mlp_ln(x, w_up, w_down, g, b, alpha, tm, tf):
    m, d = x.shape
    ff = w_up.shape[1]
    return pl.pallas_call(
        functools.partial(_mlp_kernel, alpha),
        grid=(m // tm, ff // tf),
        in_specs=[pl.BlockSpec((tm, d), lambda i, f: (i, 0)),
                  pl.BlockSpec((d, tf), lambda i, f: (0, f)),
                  pl.BlockSpec((tf, d), lambda i, f: (f, 0)),
                  pl.BlockSpec((1, d), lambda i, f: (0, 0)),
                  pl.BlockSpec((1, d), lambda i, f: (0, 0))],
        out_specs=pl.BlockSpec((tm, d), lambda i, f: (i, 0)),
        out_shape=jax.ShapeDtypeStruct((m, d), F32),
        scratch_shapes=[pltpu.VMEM((tm, d), F32)],
        compiler_params=_params("parallel", "arbitrary"),
        name="mlp_ln",
    )(x, w_up, w_down, g, b)


def _kvq_kernel(x_ref, w_ref, k_ref, v_ref, q_ref):
    e = k_ref.shape[1]
    xb = x_ref[...].astype(BF16)
    k_ref[...] = jnp.dot(xb, w_ref[:, 0:e], preferred_element_type=F32)
    v_ref[...] = jnp.dot(xb, w_ref[:, e:2 * e], preferred_element_type=F32)
    q_ref[...] = jnp.dot(xb, w_ref[:, 2 * e:3 * e], preferred_element_type=F32)


def _kvq(x, w_kvq, tm):
    m, d = x.shape
    e = w_kvq.shape[1] // 3
    row = lambda i: (i, 0)
    return pl.pallas_call(
        _kvq_kernel,
        grid=(m // tm,),
        in_specs=[pl.BlockSpec((tm, d), row), pl.BlockSpec((d, 3 * e), lambda i: (0, 0))],
        out_specs=[pl.BlockSpec((tm, e), row)] * 3,
        out_shape=[jax.ShapeDtypeStruct((m, e), F32)] * 3,
        compiler_params=_params("parallel"),
        name="kvq_proj",
    )(x, w_kvq)


def _kvq_prompt_kernel(x_ref, wk_ref, wvt_ref, wqt_ref, k_ref, kt_ref, vt_ref, qt_ref, km_ref):
    xb = x_ref[...].astype(BF16)
    k = jnp.dot(xb, wk_ref[...], preferred_element_type=F32)
    k_ref[...] = k
    kt_ref[0] = k.T
    vt_ref[0] = _bdot_nt(wvt_ref[...], xb)
    qt_ref[0] = _bdot_nt(wqt_ref[...], xb)
    km_ref[0] = jnp.sum(k, axis=0, keepdims=True) * (1.0 / k.shape[0])


def _kvq_prompt(x, w_k, w_vt, w_qt, b, l):
    m, d = x.shape
    e = w_k.shape[1]
    nb = l // MOBA_BLOCK
    fixed = lambda i, j: (0, 0)
    chan = pl.BlockSpec((1, e, MOBA_BLOCK), lambda i, j: (i, 0, j))
    chan_shape = jax.ShapeDtypeStruct((b, e, l), F32)
    return pl.pallas_call(
        _kvq_prompt_kernel,
        grid=(b, nb),
        in_specs=[pl.BlockSpec((MOBA_BLOCK, d), lambda i, j: (i * nb + j, 0)),
                  pl.BlockSpec((d, e), fixed), pl.BlockSpec((e, d), fixed),
                  pl.BlockSpec((e, d), fixed)],
        out_specs=[pl.BlockSpec((MOBA_BLOCK, e), lambda i, j: (i * nb + j, 0)), chan, chan, chan,
                   pl.BlockSpec((1, 1, e), lambda i, j: (i * nb + j, 0, 0))],
        out_shape=[jax.ShapeDtypeStruct((m, e), F32), chan_shape, chan_shape, chan_shape,
                   jax.ShapeDtypeStruct((m // MOBA_BLOCK, 1, e), F32)],
        compiler_params=_params("parallel", "parallel"),
        name="kvq_prompt",
    )(x, w_k, w_vt, w_qt)


def _gate_terms(ab, a_log, dt_bias):
    return -jnp.exp(a_log) * _softplus(ab + dt_bias)


def _gdn_in_kernel(x_ref, w_ref, wab_ref, wabt_ref, wc_ref, alog_r, dtb_r, alog_c, dtb_c,
                   q_ref, k_ref, v_ref, z_ref, gcol_ref, grow_ref, cst_ref,
                   ext_scr):
    t = pl.program_id(1)
    tm = x_ref.shape[1]
    hw = q_ref.shape[1]
    kw = wc_ref.shape[0]
    dk = hw // GDN_HEADS
    xb = x_ref[0].astype(BF16)

    @pl.when(t == 0)
    def _():
        ext_scr[:, 0:SUBLANES, :] = jnp.zeros((3, SUBLANES, hw), F32)

    outs = (q_ref, k_ref, v_ref)
    bw = 2 * dk
    for s in range(3):
        for c in range(hw // bw):
            cols = slice(s * hw + c * bw, s * hw + (c + 1) * bw)
            blk = slice(c * bw, (c + 1) * bw)
            pre = jnp.dot(xb, w_ref[:, cols], preferred_element_type=F32)
            ext_scr[s, SUBLANES:SUBLANES + tm, blk] = pre
            conv = pre * wc_ref[kw - 1:kw, cols]
            for sh in range(1, kw):
                conv = conv + (ext_scr[s, pl.ds(SUBLANES - sh, tm), blk]
                               * wc_ref[kw - 1 - sh:kw - sh, cols])
            act = _silu(conv)
            if s == 2:
                outs[s][:, blk] = act
            else:
                scale = dk ** -0.5 if s == 0 else 1.0
                for h in range(bw // dk):
                    sl = act[:, h * dk:(h + 1) * dk]
                    ss = jnp.sum(sl * sl, -1, keepdims=True)
                    outs[s][:, c * bw + h * dk:c * bw + (h + 1) * dk] = (
                        sl * (lax.rsqrt(ss + RMS_EPS) * scale))

    z_ref[...] = jnp.dot(xb, w_ref[:, 3 * hw:4 * hw], preferred_element_type=F32)

    ab = jnp.dot(xb, wab_ref[...], preferred_element_type=F32)
    lane = lax.broadcasted_iota(jnp.int32, ab.shape, 1)
    g = jnp.where(lane < GDN_HEADS, _gate_terms(ab, alog_r[...], dtb_r[...]), 0.0)
    gcol_ref[...] = jnp.where(lane < GDN_HEADS, _chunk_cumsum(g, 0), jax.nn.sigmoid(ab))

    abt = _bdot_nt(wabt_ref[...], xb)
    gct = _chunk_cumsum(_gate_terms(abt, alog_c[...], dtb_c[...]), 1)
    for c in range(tm // GDN_CHUNK):
        grow_ref[0, c] = gct[0:GDN_HEADS, c * GDN_CHUNK:(c + 1) * GDN_CHUNK]

    @pl.when(t == pl.num_programs(1) - 1)
    def _():
        for s in range(3):
            cst_ref[0, :, s * hw:(s + 1) * hw] = ext_scr[
                s, pl.ds(SUBLANES + tm - (kw - 1), kw - 1), :]

    for s in range(3):
        ext_scr[s, 0:SUBLANES, :] = ext_scr[s, tm:tm + SUBLANES, :]


def _gdn_in(x, w_qkvz, w_ab, w_abt, w_conv, alog_r, dtb_r, alog_c, dtb_c, tm):
    b, l, d = x.shape
    hw = w_qkvz.shape[1] // 4
    kw = w_conv.shape[0]
    nt = l // tm
    cpt = tm // GDN_CHUNK
    row = lambda i, t: (i * nt + t, 0)
    fixed = lambda i, t: (0, 0)
    act = jax.ShapeDtypeStruct((b * l, hw), F32)
    return pl.pallas_call(
        _gdn_in_kernel,
        grid=(b, nt),
        in_specs=[pl.BlockSpec((1, tm, d), lambda i, t: (i, t, 0)),
                  pl.BlockSpec(w_qkvz.shape, fixed), pl.BlockSpec(w_ab.shape, fixed),
                  pl.BlockSpec(w_abt.shape, fixed), pl.BlockSpec(w_conv.shape, fixed),
                  pl.BlockSpec(alog_r.shape, fixed), pl.BlockSpec(dtb_r.shape, fixed),
                  pl.BlockSpec(alog_c.shape, fixed), pl.BlockSpec(dtb_c.shape, fixed)],
        out_specs=[pl.BlockSpec((tm, hw), row)] * 4 + [
            pl.BlockSpec((tm, LANES), row),
            pl.BlockSpec((1, cpt, GDN_HEADS, GDN_CHUNK), lambda i, t: (i, t, 0, 0)),
            pl.BlockSpec((1, kw - 1, 3 * hw), lambda i, t: (i, 0, 0))],
        out_shape=[act] * 4 + [
            jax.ShapeDtypeStruct((b * l, LANES), F32),
            jax.ShapeDtypeStruct((b, l // GDN_CHUNK, GDN_HEADS, GDN_CHUNK), F32),
            jax.ShapeDtypeStruct((b, kw - 1, 3 * hw), F32)],
        scratch_shapes=[pltpu.VMEM((3, SUBLANES + tm, hw), F32)],
        compiler_params=_params("parallel", "arbitrary"),
        name="gdn_in",
    )(x, w_qkvz, w_ab, w_abt, w_conv, alog_r, dtb_r, alog_c, dtb_c)


def _gdn_chunk_kernel(alpha, q_ref, k_ref, v_ref, z_ref, gcol_ref, grow_ref, wn_ref, x_ref, wo_ref,
                      lng_ref, lnb_ref, y_ref, sfin_ref, s_scr, o_scr):
    c = pl.program_id(1)
    cs = GDN_CHUNK
    nseq, tr, hw = q_ref.shape[1:]
    dk = hw // GDN_HEADS
    cpb = tr // cs

    @pl.when(c == 0)
    def _():
        s_scr[...] = jnp.zeros_like(s_scr)

    ri = lax.broadcasted_iota(jnp.int32, (cs, cs), 0)
    ci = lax.broadcasted_iota(jnp.int32, (cs, cs), 1)
    incl = ri >= ci
    strict = ri > ci
    eye = (ri == ci).astype(F32)
    heads = range(GDN_HEADS)
    units = [(s, j, h) for s in range(nseq) for j in range(cpb) for h in heads]
    chains = [(s, h) for s in range(nseq) for h in heads]
    rows = {j: slice(j * cs, (j + 1) * cs) for j in range(cpb)}
    cols = {h: slice(h * dk, (h + 1) * dk) for h in heads}
    each = lambda f: {unit: f(*unit) for unit in units}
    gcol = {(s, j): gcol_ref[0, s, rows[j], :] for s in range(nseq) for j in range(cpb)}
    grow = {(s, j): grow_ref[0, s, j] for s in range(nseq) for j in range(cpb)}
    q = each(lambda s, j, h: q_ref[0, s, rows[j], cols[h]])
    k = each(lambda s, j, h: k_ref[0, s, rows[j], cols[h]])
    qb = each(lambda s, j, h: q[s, j, h].astype(BF16))
    kbf = each(lambda s, j, h: k[s, j, h].astype(BF16))
    gcc = each(lambda s, j, h: gcol[s, j][:, h:h + 1])
    beta = each(lambda s, j, h: gcol[s, j][:, GDN_HEADS + h:GDN_HEADS + h + 1])
    decay = each(lambda s, j, h: jnp.exp(
        jnp.where(incl, gcc[s, j, h] - grow[s, j][h:h + 1, :], NEG)))
    eg = each(lambda s, j, h: jnp.exp(gcc[s, j, h]))
    glast = each(lambda s, j, h: gcc[s, j, h][cs - 1:cs, :])
    kb = each(lambda s, j, h: k[s, j, h] * beta[s, j, h])
    kk = each(lambda s, j, h: _bdot_nt(kb[s, j, h], kbf[s, j, h]))
    qk = each(lambda s, j, h: _bdot_nt(qb[s, j, h], kbf[s, j, h]))
    p = each(lambda s, j, h: jnp.where(strict, -kk[s, j, h] * decay[s, j, h], 0.0))
    tinv = each(lambda s, j, h: eye + p[s, j, h])
    for _ in range(int(math.log2(cs)) - 1):
        pb = each(lambda s, j, h: p[s, j, h].astype(BF16))
        p = each(lambda s, j, h: _bdot(pb[s, j, h], pb[s, j, h]))
        tinv = each(lambda s, j, h: tinv[s, j, h] + _bdot(tinv[s, j, h], p[s, j, h]))
    tb = each(lambda s, j, h: tinv[s, j, h].astype(BF16))
    u = each(lambda s, j, h: _bdot(tb[s, j, h], v_ref[0, s, rows[j], cols[h]] * beta[s, j, h]))
    w = each(lambda s, j, h: _bdot(tb[s, j, h], kb[s, j, h] * eg[s, j, h]).astype(BF16))
    qg = each(lambda s, j, h: (q[s, j, h] * eg[s, j, h]).astype(BF16))
    attn = each(lambda s, j, h: (qk[s, j, h] * decay[s, j, h]).astype(BF16))
    ktail = each(lambda s, j, h: (k[s, j, h]
                                  * jnp.exp(glast[s, j, h] - gcc[s, j, h])).astype(BF16))
    st = {(s, h): s_scr[s, h] for s, h in chains}
    for j in range(cpb):
        sb = {(s, h): st[s, h].astype(BF16) for s, h in chains}
        v_new = {(s, h): (u[s, j, h] - _bdot(w[s, j, h], sb[s, h])).astype(BF16)
                 for s, h in chains}
        o = {(s, h): _bdot(qg[s, j, h], sb[s, h]) + _bdot(attn[s, j, h], v_new[s, h])
             for s, h in chains}
        st = {(s, h): (st[s, h] * jnp.exp(glast[s, j, h])
                       + _bdot_tn(ktail[s, j, h], v_new[s, h])) for s, h in chains}
        for s, h in chains:
            on = (o[s, h] * lax.rsqrt(jnp.mean(o[s, h] * o[s, h], -1, keepdims=True) + RMS_EPS)
                  * wn_ref[...])
            o_scr[s, rows[j], cols[h]] = (
                on * _silu(z_ref[0, s, rows[j], cols[h]])).astype(o_scr.dtype)
    for s, h in chains:
        s_scr[s, h] = st[s, h]
    for s in range(nseq):
        mix = jnp.dot(o_scr[s], wo_ref[...], preferred_element_type=F32)
        y_ref[0, s] = _layer_norm(alpha * x_ref[0, s] + mix, lng_ref[...], lnb_ref[...])

    @pl.when(c == pl.num_programs(1) - 1)
    def _():
        sfin_ref[0] = s_scr[...]


def _gdn_chunks(q, k, v, z, gcol, grow, w_onorm, x, w_o, ln_g, ln_b, alpha, b, l, nseq, cpb):
    hw = q.shape[1]
    d = x.shape[1]
    dk = hw // GDN_HEADS
    tr = cpb * GDN_CHUNK
    nc = l // tr
    assert b % nseq == 0 and l % tr == 0
    seqs = lambda a: a.reshape((b // nseq, nseq, l) + a.shape[1:])
    tile = lambda w_: pl.BlockSpec((1, nseq, tr, w_), lambda i, c: (i, 0, c, 0))
    fixed = lambda i, c: (0, 0)
    y, s_fin = pl.pallas_call(
        functools.partial(_gdn_chunk_kernel, alpha),
        grid=(b // nseq, nc),
        in_specs=[tile(hw)] * 4 + [
            tile(LANES),
            pl.BlockSpec((1, nseq, cpb, GDN_HEADS, GDN_CHUNK), lambda i, c: (i, 0, c, 0, 0)),
            pl.BlockSpec((1, dk), fixed), tile(d), pl.BlockSpec((hw, d), fixed),
            pl.BlockSpec((1, d), fixed), pl.BlockSpec((1, d), fixed)],
        out_specs=[tile(d),
                   pl.BlockSpec((1, nseq, GDN_HEADS, dk, dk), lambda i, c: (i, 0, 0, 0, 0))],
        out_shape=[jax.ShapeDtypeStruct((b // nseq, nseq, l, d), F32),
                   jax.ShapeDtypeStruct((b // nseq, nseq, GDN_HEADS, dk, dk), F32)],
        scratch_shapes=[pltpu.VMEM((nseq, GDN_HEADS, dk, dk), F32),
                        pltpu.VMEM((nseq, tr, hw), BF16)],
        compiler_params=_params("parallel", "arbitrary"),
        name="gdn_chunks",
    )(seqs(q), seqs(k), seqs(v), seqs(z), seqs(gcol),
      grow.reshape(b // nseq, nseq, l // GDN_CHUNK, GDN_HEADS, GDN_CHUNK), w_onorm, seqs(x), w_o,
      ln_g, ln_b)
    return y.reshape(b * l, d), s_fin.reshape(b, GDN_HEADS, dk, dk)


def _gdn_in_sample_kernel(x_ref, w_ref, wab_ref, wc_ref, alog_r, dtb_r, sc_ref,
                          q_ref, k_ref, v_ref, z_ref, gate_ref, cst_ref):
    hw = q_ref.shape[1]
    kw = wc_ref.shape[0]
    dk = hw // GDN_HEADS
    xb = x_ref[...].astype(BF16)
    outs = (q_ref, k_ref, v_ref)
    for s in range(3):
        cols = slice(s * hw, (s + 1) * hw)
        pre = jnp.dot(xb, w_ref[:, cols], preferred_element_type=F32)
        conv = pre * wc_ref[kw - 1:kw, cols]
        for j in range(kw - 1):
            conv = conv + sc_ref[j, :, cols] * wc_ref[j:j + 1, cols]
        cst_ref[kw - 2, :, cols] = pre
        act = _silu(conv)
        if s == 2:
            outs[s][...] = act
        else:
            scale = dk ** -0.5 if s == 0 else 1.0
            for h in range(GDN_HEADS):
                hs = slice(h * dk, (h + 1) * dk)
                sl = act[:, hs]
                ss = jnp.sum(sl * sl, -1, keepdims=True)
                outs[s][:, hs] = sl * (lax.rsqrt(ss + RMS_EPS) * scale)
    cst_ref[0:kw - 2] = sc_ref[1:kw - 1]
    z_ref[...] = jnp.dot(xb, w_ref[:, 3 * hw:4 * hw], preferred_element_type=F32)
    ab = jnp.dot(xb, wab_ref[...], preferred_element_type=F32)
    lane = lax.broadcasted_iota(jnp.int32, ab.shape, 1)
    gate_ref[...] = jnp.where(lane < GDN_HEADS, _gate_terms(ab, alog_r[...], dtb_r[...]),
                              jax.nn.sigmoid(ab))


def _gdn_in_sample(x, w_qkvz, w_ab, w_conv, alog_r, dtb_r, conv_state):
    m = x.shape[0]
    hw = w_qkvz.shape[1] // 4
    act = jax.ShapeDtypeStruct((m, hw), F32)
    return pl.pallas_call(
        _gdn_in_sample_kernel,
        out_shape=[act] * 4 + [jax.ShapeDtypeStruct((m, LANES), F32),
                               jax.ShapeDtypeStruct(conv_state.shape, F32)],
        compiler_params=pltpu.CompilerParams(vmem_limit_bytes=VMEM_LIMIT_BYTES),
        name="gdn_in_sample",
    )(x, w_qkvz, w_ab, w_conv, alog_r, dtb_r, conv_state)


def _gdn_step_kernel(q_ref, k_ref, v_ref, z_ref, gate_ref, wn_ref, s_ref, o_ref, snew_ref):
    dk = s_ref.shape[2]
    gate = gate_ref[0]
    for h in range(GDN_HEADS):
        hs = slice(h * dk, (h + 1) * dk)
        q = q_ref[0, :, hs]
        k = k_ref[0, :, hs]
        v = v_ref[0, :, hs]
        eg = jnp.exp(gate[:, h:h + 1])
        beta = gate[:, GDN_HEADS + h:GDN_HEADS + h + 1]
        s = s_ref[0, h]
        kc = jnp.broadcast_to(k, (dk, dk)).T
        qc = jnp.broadcast_to(q, (dk, dk)).T
        ks = jnp.sum(kc * s, axis=0, keepdims=True)
        qs = jnp.sum(qc * s, axis=0, keepdims=True)
        v_new = beta * (v - eg * ks)
        o = eg * qs + jnp.sum(q * k, -1, keepdims=True) * v_new
        snew_ref[0, h] = eg * s + kc * v_new
        o = o * lax.rsqrt(jnp.mean(o * o, -1, keepdims=True) + RMS_EPS) * wn_ref[...]
        o_ref[0, :, hs] = o * _silu(z_ref[0, :, hs])


def _gdn_step(q, k, v, z, gate, w_onorm, state):
    m, hw = q.shape
    dk = hw // GDN_HEADS
    r3 = lambda a: a.reshape(m, 1, a.shape[1])
    tok = pl.BlockSpec((1, 1, hw), lambda i: (i, 0, 0))
    st = pl.BlockSpec((1, GDN_HEADS, dk, dk), lambda i: (i, 0, 0, 0))
    o, s_new = pl.pallas_call(
        _gdn_step_kernel,
        grid=(m,),
        in_specs=[tok] * 4 + [pl.BlockSpec((1, 1, LANES), lambda i: (i, 0, 0)),
                              pl.BlockSpec((1, dk), lambda i: (0, 0)), st],
        out_specs=[tok, st],
        out_shape=[jax.ShapeDtypeStruct((m, 1, hw), F32),
                   jax.ShapeDtypeStruct(state.shape, F32)],
        compiler_params=_params("parallel"),
        name="gdn_step",
    )(r3(q), r3(k), r3(v), r3(z), r3(gate), w_onorm, state)
    return o.reshape(m, hw), s_new


def _zigzag(step, n):
    return jnp.where(step % 2 == 0, step // 2, n - 1 - step // 2)


def _select_init(t, q_ref, qcol_scr, gate_scr):
    @pl.when(t == 0)
    def _():
        e, page = qcol_scr.shape
        qcol_scr[...] = jnp.broadcast_to(q_ref[0], (page, e)).T
        gate_scr[...] = jnp.zeros_like(gate_scr)


def _select_fold(ppb, t, pages, qcol_scr, gate_scr):
    e, page = qcol_scr.shape
    hd = e // MOBA_HEADS
    bps = len(pages) // ppb
    lane = lax.broadcasted_iota(jnp.int32, (MOBA_HEADS, LANES), 1)
    gate = gate_scr[...]
    qcol = qcol_scr[...]
    for blk in range(bps):
        tot = pages[blk * ppb][0]
        for j in range(1, ppb):
            tot = tot + pages[blk * ppb + j][0]
        per_head = jnp.sum((tot * qcol).reshape(MOBA_HEADS, hd, page), axis=1)
        g = jnp.sum(per_head, -1, keepdims=True) * (1.0 / MOBA_BLOCK)
        gate = jnp.where(lane == t * bps + blk, g, gate)
    gate_scr[...] = gate


def _select_pick(n_sel, nbf, last, idx_ref, gate_scr):
    @pl.when(last)
    def _():
        lane = lax.broadcasted_iota(jnp.int32, (MOBA_HEADS, LANES), 1)
        lane_f = lane.astype(F32)
        g = jnp.where(lane < nbf, gate_scr[...], -jnp.inf)
        picks = jnp.zeros((MOBA_HEADS, LANES), F32)
        for r in range(n_sel):
            best = jnp.max(g, -1, keepdims=True)
            idx = jnp.min(jnp.where(g == best, lane_f, float(LANES)), -1, keepdims=True)
            picks = jnp.where(lane == r, idx, picks)
            g = jnp.where(lane_f == idx, -jnp.inf, g)
        idx_ref[0] = picks.astype(jnp.int32)


def _moba_prompt_kernel(n_sel, n_sel_s, ppb, nbf, pt_ref, slopes_ref, qt_lo_ref, qt_hi_ref, k_ref,
                        vt_ref, km_ref, qs_ref, *rest):
    del pt_ref
    pages = rest[:-4]
    o_ref, idx_ref, qcol_scr, gate_scr = rest[-4:]
    p = pl.program_id(1)
    half = pl.num_programs(2)
    a = _zigzag(pl.program_id(2), half)
    step = (pl.program_id(0) * pl.num_programs(1) + p) * pl.num_programs(2) + pl.program_id(2)
    sps = nbf * ppb // len(pages)
    t = step % sps
    _select_init(t, qs_ref, qcol_scr, gate_scr)
    bs = MOBA_BLOCK
    hd = LANES // 2
    nb = km_ref.shape[1]
    nbp = -(-nb // SUBLANES) * SUBLANES
    rel = (lax.broadcasted_iota(jnp.int32, (bs, bs), 1)
           - lax.broadcasted_iota(jnp.int32, (bs, bs), 0))
    aug_lane = lax.broadcasted_iota(jnp.int32, (bs, LANES), 1)
    k_idx = lax.broadcasted_iota(jnp.int32, (bs, LANES), 0).astype(F32)
    k_aug = jnp.where(aug_lane < 3, k_idx, jnp.where(aug_lane < 6, 1.0, 0.0)).astype(BF16)

    def query_side(qt_ref, i):
        qt = qt_ref[0]
        km = km_ref[0]
        if nbp > nb:
            km = jnp.concatenate([km, jnp.zeros((nbp - nb, LANES), F32)], axis=0)
        k_hi, k_mid, _ = _split3(km)
        chan = lax.broadcasted_iota(jnp.int32, (LANES, bs), 0)
        blk = lax.broadcasted_iota(jnp.int32, (nbp, bs), 0)
        q_idx = lax.broadcasted_iota(jnp.int32, (1, bs), 1).astype(F32)
        aug_row = lax.broadcasted_iota(jnp.int32, (2 * SUBLANES, bs), 0)
        wqs, keeps, slope_h = [], [], []
        for hh in range(2):
            qh = jnp.where((chan // hd) == hh, qt, 0.0)
            q_hi, q_mid, _ = _split3(qh)
            gate = _bdot(k_hi, q_hi) + _bdot(k_hi, q_mid) + _bdot(k_mid, q_hi)
            cnt = jnp.zeros((nbp, bs), jnp.int32)
            for m in range(i):
                gm = gate[m:m + 1, :]
                beats = (gm > gate) | ((gm == gate) & (m < blk))
                cnt = cnt + jnp.where(beats, 1, 0)
            keeps.append((blk < i) & (cnt < n_sel))
            slope2 = slopes_ref[2 * p + hh] * LOG2E
            slope_h.append(slope2)
            s_parts = _split3(jnp.full((1, bs), slope2, F32))
            t_parts = _split3(-slope2 * q_idx)
            aug = jnp.zeros((2 * SUBLANES, bs), F32)
            for r, part in enumerate(s_parts + t_parts):
                aug = jnp.where(aug_row == r, part.astype(F32), aug)
            wqs.append(jnp.concatenate(
                [(qh * (hd ** -0.5 * LOG2E)).astype(BF16), aug.astype(BF16),
                 jnp.zeros((LANES - 2 * SUBLANES, bs), BF16)], axis=0))
        return wqs, keeps, slope_h

    def attend(n, qt_ref, slot):
        wqs, keeps, slope_h = query_side(qt_ref, n - 1)
        pair = range(2)
        kcat = jnp.concatenate([k_ref[0:n * bs, :].astype(BF16),
                                jnp.concatenate([k_aug] * n, axis=0)], axis=1)
        s_all = [jnp.dot(kcat, wqs[hh], preferred_element_type=F32) for hh in pair]
        us, shifts = [], []
        for hh in pair:
            u = [s_all[hh][j * bs:(j + 1) * bs, :] for j in range(n - 1)]
            u.append(jnp.where(rel >= 0, s_all[hh][(n - 1) * bs:n * bs, :], NEG))
            rows = [jnp.where(keeps[hh][j:j + 1, :], -slope_h[hh] * float((n - 1 - j) * bs), NEG)
                    for j in range(n - 1)] + [jnp.zeros((1, bs), F32)]
            mx = functools.reduce(
                jnp.maximum, [jnp.max(u[j], 0, keepdims=True) + rows[j] for j in range(n)])
            us.append(u)
            shifts.append([mx - rows[j] for j in range(n)])
        ps = [[jnp.exp2(us[hh][j] - shifts[hh][j]) for j in range(n)] for hh in pair]
        den = [sum(jnp.sum(pm, 0, keepdims=True) for pm in ps[hh]) for hh in pair]
        pcat = [jnp.concatenate([pm.astype(BF16) for pm in ps[hh]], axis=0) for hh in pair]
        acc = [_bdot(vt_ref[0, hh * hd:(hh + 1) * hd, 0:n * bs], pcat[hh]) for hh in pair]
        o_ref[0, slot, 0] = jnp.concatenate([acc[hh] / den[hh] for hh in pair],
                                            axis=0).T.astype(o_ref.dtype)

    def attend_two(a_static):
        _select_fold(ppb, t, pages, qcol_scr, gate_scr)
        attend(a_static + 1, qt_lo_ref, 0)
        attend(a_static + 1 + nb // 2, qt_hi_ref, 1)

    for a_static in range(nb // 2):
        pl.when(a == a_static)(functools.partial(attend_two, a_static))
    _select_pick(n_sel_s, nbf, t == sps - 1, idx_ref, gate_scr)


def _moba_prompt(qt, k, vt, k_mean, slopes, b, l, q_s, cache_kt, page_table, n_sel_s):
    e = k.shape[1]
    nb = l // MOBA_BLOCK
    n_sel = min(MOBA_TOPK, (l - 1) // MOBA_BLOCK)
    npair = e // LANES
    m = q_s.shape[0]
    _, _, page = cache_kt.shape
    n_pages = page_table.shape[1]
    ppb = MOBA_BLOCK // page
    nbf = n_pages // ppb
    assert nb % 2 == 0
    half = nb // 2
    steps = b * npair * half
    assert steps % m == 0 and nbf <= LANES and page == LANES
    sps = steps // m
    assert n_pages % sps == 0 and (n_pages // sps) % ppb == 0
    pps = n_pages // sps
    seq = lambda bi, p, i: ((bi * npair + p) * half + i) // sps
    tstep = lambda bi, p, i: ((bi * npair + p) * half + i) % sps

    def page_spec(j):
        return pl.BlockSpec(
            (1, e, page), lambda bi, p, i, pt: (pt[seq(bi, p, i), tstep(bi, p, i) * pps + j], 0, 0))

    def qt_spec(upper):
        return pl.BlockSpec((1, LANES, MOBA_BLOCK),
                            lambda bi, p, i, pt: (bi, p, _zigzag(i, half) + upper * half))

    o, idx = pl.pallas_call(
        functools.partial(_moba_prompt_kernel, n_sel, n_sel_s, ppb, nbf),
        grid_spec=pltpu.PrefetchScalarGridSpec(
            num_scalar_prefetch=1,
            grid=(b, npair, half),
            in_specs=[pl.BlockSpec(memory_space=pltpu.SMEM), qt_spec(0), qt_spec(1),
                      pl.BlockSpec((l, LANES), lambda bi, p, i, pt: (bi, p)),
                      pl.BlockSpec((1, LANES, l), lambda bi, p, i, pt: (bi, p, 0)),
                      pl.BlockSpec((1, nb, LANES), lambda bi, p, i, pt: (bi, 0, p)),
                      pl.BlockSpec((1, 1, e), lambda bi, p, i, pt: (seq(bi, p, i), 0, 0))]
            + [page_spec(j) for j in range(pps)],
            out_specs=[pl.BlockSpec((1, 2, 1, MOBA_BLOCK, LANES),
                                    lambda bi, p, i, pt: (bi, 0, _zigzag(i, half), 0, p)),
                       pl.BlockSpec((1, MOBA_HEADS, LANES),
                                    lambda bi, p, i, pt: (seq(bi, p, i), 0, 0))],
            scratch_shapes=[pltpu.VMEM((e, page), F32), pltpu.VMEM((MOBA_HEADS, LANES), F32)]),
        out_shape=[jax.ShapeDtypeStruct((b, 2, half, MOBA_BLOCK, e), BF16),
                   jax.ShapeDtypeStruct((m, MOBA_HEADS, LANES), jnp.int32)],
        compiler_params=_params("arbitrary", "arbitrary", "arbitrary"),
        name="moba_prompt",
    )(page_table, slopes, qt, qt, k, vt, k_mean, q_s.reshape(m, 1, e), *([cache_kt] * pps))
    return o.reshape(b * l, e), idx


def _moba_sample_kernel(n_sel, ppb, past, hps, pt_ref, sel_ref, slopes_ref, q_ref, kn_ref, vn_ref,
                        *refs):
    del pt_ref
    tph = n_sel * ppb
    k_tiles = refs[:hps * tph]
    v_tiles = refs[hps * tph:2 * hps * tph]
    o_ref = refs[2 * hps * tph]
    bi = pl.program_id(0)
    grp = pl.program_id(1)
    hd, page = k_tiles[0].shape[1:]
    pos_in_page = lax.broadcasted_iota(jnp.int32, (SUBLANES, page), 1)
    q = q_ref[0]
    kn = kn_ref[0]
    vn = vn_ref[0]
    heads = range(hps)
    tiles = range(tph)
    ls = [slice(hh * hd, (hh + 1) * hd) for hh in heads]
    qh = [q[:, ls[hh]] * hd ** -0.5 for hh in heads]
    q8 = [jnp.broadcast_to(qh[hh], (SUBLANES, hd)).astype(BF16) for hh in heads]
    s_own = [jnp.sum(qh[hh] * kn[:, ls[hh]], -1, keepdims=True) for hh in heads]
    raw = [[jnp.dot(q8[hh], k_tiles[hh * tph + n][0].astype(BF16), preferred_element_type=F32)
            for n in tiles] for hh in heads]
    scores = []
    for hh in heads:
        h = grp * hps + hh
        slope = slopes_ref[h]
        row = []
        for n in tiles:
            blk = sel_ref[bi, h * n_sel + n // ppb]
            dist = (past - blk * MOBA_BLOCK - (n % ppb) * page - pos_in_page).astype(F32)
            row.append(raw[hh][n] - slope * dist)
        scores.append(row)
    mx = [functools.reduce(jnp.maximum,
                           [jnp.max(sc, -1, keepdims=True)[0:1] for sc in scores[hh]], s_own[hh])
          for hh in heads]
    pm = [[jnp.exp(sc - mx[hh]) for sc in scores[hh]] for hh in heads]
    pv = [[_bdot_nt(pm[hh][n], v_tiles[hh * tph + n][0])[0:1] for n in tiles] for hh in heads]
    outs = []
    for hh in heads:
        p_own = jnp.exp(s_own[hh] - mx[hh])
        den = p_own + sum(jnp.sum(pm[hh][n], -1, keepdims=True)[0:1] for n in tiles)
        acc = p_own * vn[:, ls[hh]] + sum(pv[hh])
        outs.append(acc / den)
    o_ref[0] = jnp.concatenate(outs, axis=-1)


def _moba_sample(q, k_new, v_new, cache_kt, cache_vt, page_table, sel, slopes, n_sel):
    m, e = q.shape
    _, _, page = cache_kt.shape
    hd = e // MOBA_HEADS
    ppb = MOBA_BLOCK // page
    past = page_table.shape[1] * page
    hps = 8
    assert MOBA_HEADS % hps == 0 and (hps * hd) % LANES == 0
    r3 = lambda a: a.reshape(m, 1, e)
    tok = pl.BlockSpec((1, 1, hps * hd), lambda bi, g, pt, sl: (bi, 0, g))

    def tile_spec(hh, s, j):
        def index(bi, g, pt, sl):
            h = g * hps + hh
            return pt[bi, sl[bi, h * n_sel + s] * ppb + j], h, 0
        return pl.BlockSpec((1, hd, page), index)

    tiles = [tile_spec(hh, s, j) for hh in range(hps) for s in range(n_sel) for j in range(ppb)]
    out = pl.pallas_call(
        functools.partial(_moba_sample_kernel, n_sel, ppb, past, hps),
        grid_spec=pltpu.PrefetchScalarGridSpec(
            num_scalar_prefetch=2,
            grid=(m, MOBA_HEADS // hps),
            in_specs=[pl.BlockSpec(memory_space=pltpu.SMEM), tok, tok, tok] + tiles + tiles,
            out_specs=tok),
        out_shape=jax.ShapeDtypeStruct((m, 1, e), F32),
        compiler_params=_params("parallel", "parallel"),
        name="moba_sample",
    )(page_table, sel, slopes, r3(q), r3(k_new), r3(v_new),
      *([cache_kt] * len(tiles)), *([cache_vt] * len(tiles)))
    return out.reshape(m, e)


def _row_tile(m, cap):
    t = min(m, cap)
    assert m % t == 0
    return t


def kernel(x_prompt, x_sample, cache_k, cache_v, state_delta, state_conv, page_table,
           gdn_w_in, gdn_w_conv, gdn_a_log, gdn_dt_bias, gdn_w_onorm, gdn_w_o,
           w_kv, moba_w_q, moba_w_o, mlp_w_up, mlp_w_down, ln_g, ln_b):
    bp, lp, d = x_prompt.shape
    bs, ls, _ = x_sample.shape
    depth = mlp_w_up.shape[0]
    n_a = gdn_w_in.shape[0]
    alpha = (2 * depth) ** 0.25
    n_pool, page = cache_k.shape[0], cache_k.shape[1]
    e_kv = cache_k.shape[2] * cache_k.shape[3]
    hw = gdn_w_o.shape[1]
    conv_ch = gdn_w_conv.shape[2]
    kw = gdn_w_conv.shape[1]
    n_pages = page_table.shape[1]
    assert ls == 1, "sample stream handles one new token per sequence"
    assert conv_ch == 3 * hw and kw - 1 <= SUBLANES and kw >= 3
    assert lp % MOBA_BLOCK == 0 and MOBA_BLOCK % GDN_CHUNK == 0 and MOBA_BLOCK % page == 0
    assert (n_pages * page) % MOBA_BLOCK == 0, "past length must end on a MoBA block boundary"
    assert e_kv == MOBA_HEADS * (LANES // 2)
    assert depth - n_a == 1, "one MoBA layer reads the shared K/V"
    nbf = n_pages * page // MOBA_BLOCK
    n_sel_s = min(MOBA_TOPK, nbf)
    assert n_sel_s >= 1

    xp = x_prompt.reshape(bp * lp, d)
    xs = x_sample.reshape(bs, d)
    tm_p = _row_tile(bp * lp, 512)
    tm_s = _row_tile(bs, 512)
    row = lambda a: a.reshape(1, -1).astype(F32)
    heads = jnp.arange(1, MOBA_HEADS + 1, dtype=F32)
    slopes = jnp.exp2(-ALIBI_MAX_EXP * heads / MOBA_HEADS)
    cache_kt = cache_k.transpose(0, 2, 3, 1).reshape(n_pool, e_kv, page)
    cache_vt = cache_v.transpose(0, 2, 3, 1).reshape(n_pool, e_kv, page)

    conv_p, delta_p, conv_s, delta_s = [], [], [], []
    for layer in range(depth):
        g0, b0 = row(ln_g[layer, 0]), row(ln_b[layer, 0])
        g1, b1 = row(ln_g[layer, 1]), row(ln_b[layer, 1])
        if layer < n_a:
            w_in = gdn_w_in[layer]
            w_qkvz = w_in[:, :4 * hw].astype(BF16)
            w_gate = w_in[:, 4 * hw:]
            w_ab = jnp.pad(w_gate, ((0, 0), (0, LANES - 2 * GDN_HEADS))).astype(BF16)
            w_abt = w_gate.T.astype(BF16)
            pad_r = lambda a: jnp.pad(row(a), ((0, 0), (0, LANES - GDN_HEADS)))
            pad_c = lambda a: jnp.pad(a.astype(F32).reshape(-1, 1), ((0, GDN_HEADS), (0, 0)))
            alog_r, dtb_r = pad_r(gdn_a_log[layer]), pad_r(gdn_dt_bias[layer])
            alog_c, dtb_c = pad_c(gdn_a_log[layer]), pad_c(gdn_dt_bias[layer])
            w_conv = gdn_w_conv[layer].astype(F32)
            w_onorm = row(gdn_w_onorm[layer])
            w_o = gdn_w_o[layer].astype(BF16)

            q, k, v, z, gcol, grow, cst = _gdn_in(
                x_prompt if layer == 0 else xp.reshape(bp, lp, d),
                w_qkvz, w_ab, w_abt, w_conv, alog_r, dtb_r, alog_c, dtb_c, tm=MOBA_BLOCK)
            xp, s_fin = _gdn_chunks(q, k, v, z, gcol, grow, w_onorm, xp, w_o, g0, b0, alpha,
                                    bp, lp, nseq=2, cpb=2)
            conv_p.append(cst)
            delta_p.append(s_fin)
            mix_p = None

            q, k, v, z, gate, cst = _gdn_in_sample(
                xs, w_qkvz, w_ab, w_conv, alog_r, dtb_r,
                state_conv[layer].astype(F32).transpose(1, 0, 2))
            o, s_new = _gdn_step(q, k, v, z, gate, w_onorm, state_delta[layer].astype(F32))
            conv_s.append(cst.transpose(1, 0, 2))
            delta_s.append(s_new)
            mix_s = (o, w_o)
        else:
            w_q = moba_w_q[layer - n_a]
            k_row, kt_p, vt_p, qt_p, km_p = _kvq_prompt(
                xp, w_kv[:, :e_kv].astype(BF16), w_kv[:, e_kv:].T.astype(BF16),
                w_q.T.astype(BF16), bp, lp)
            k_s, v_s, q_s = _kvq(xs, jnp.concatenate([w_kv, w_q], axis=1).astype(BF16), tm_s)
            w_o = moba_w_o[layer - n_a].astype(BF16)
            o, sel = _moba_prompt(qt_p, k_row, vt_p, km_p.reshape(bp, lp // MOBA_BLOCK, e_kv),
                                  slopes, bp, lp, q_s, cache_kt, page_table, n_sel_s)
            mix_p = (o, w_o)
            sel = sel[:, :, :n_sel_s].reshape(bs, MOBA_HEADS * n_sel_s)
            o = _moba_sample(q_s, k_s, v_s, cache_kt, cache_vt, page_table, sel, slopes, n_sel_s)
            mix_s = (o, w_o)
        w_up = mlp_w_up[layer].astype(BF16)
        w_down = mlp_w_down[layer].astype(BF16)
        if mix_p is not None:
            xp = _proj_ln(mix_p[0], xp, mix_p[1], g0, b0, alpha, tm_p)
        xs = _proj_ln(mix_s[0], xs, mix_s[1], g0, b0, alpha, tm_s)
        xp = _mlp_ln(xp, w_up, w_down, g1, b1, alpha, tm_p, 2048)
        xs = _mlp_ln(xs, w_up, w_down, g1, b1, alpha, tm_s, 1024)

    hd = e_kv // MOBA_HEADS
    tokens_major = lambda a: a.reshape(bp, MOBA_HEADS, hd, lp).transpose(0, 3, 1, 2)
    return (xp.reshape(bp, lp, d), xs.reshape(bs, ls, d),
            tokens_major(kt_p), tokens_major(vt_p),
            jnp.stack(delta_p), jnp.stack(conv_p),
            k_s.reshape(bs, ls, MOBA_HEADS, hd), v_s.reshape(bs, ls, MOBA_HEADS, hd),
            jnp.stack(delta_s), jnp.stack(conv_s))
```

```python
import functools
import math

import jax
import jax.numpy as jnp
from jax import lax
from jax.experimental import pallas as pl
from jax.experimental.pallas import tpu as pltpu

F32 = jnp.float32
BF16 = jnp.bfloat16

GDN_HEADS = 8
GDN_CHUNK = 64
MOBA_HEADS = 16
MOBA_BLOCK = 256
MOBA_TOPK = 3
ALIBI_MAX_EXP = 16
LN_EPS = 1e-5
RMS_EPS = 1e-6

LANES = 128
SUBLANES = 8
VMEM_LIMIT_BYTES = 48 * 1024 * 1024

NEG = -1e30
LOG2E = math.log2(math.e)


def _params(*semantics):
    return pltpu.CompilerParams(dimension_semantics=semantics,
                                vmem_limit_bytes=VMEM_LIMIT_BYTES)


def _bdot(a, b):
    return jnp.dot(a.astype(BF16), b.astype(BF16), preferred_element_type=F32)


def _bdot_nt(a, b):
    return lax.dot_general(a.astype(BF16), b.astype(BF16), (((1,), (1,)), ((), ())),
                           preferred_element_type=F32)


def _bdot_tn(a, b):
    return lax.dot_general(a.astype(BF16), b.astype(BF16), (((0,), (0,)), ((), ())),
                           preferred_element_type=F32)


def _split3(a):
    hi = a.astype(BF16)
    r = a - hi.astype(F32)
    mid = r.astype(BF16)
    lo = (r - mid.astype(F32)).astype(BF16)
    return hi, mid, lo


def _layer_norm(h, g, b):
    mu = jnp.mean(h, -1, keepdims=True)
    d = h - mu
    var = jnp.mean(d * d, -1, keepdims=True)
    return d * lax.rsqrt(var + LN_EPS) * g + b


def _silu(x):
    return x * jax.nn.sigmoid(x)


def _softplus(x):
    return jnp.maximum(x, 0.0) + jnp.log(1.0 + jnp.exp(-jnp.abs(x)))


def _chunk_cumsum(g, axis):
    pos = lax.broadcasted_iota(jnp.int32, g.shape, axis) % GDN_CHUNK
    sh = 1
    while sh < GDN_CHUNK:
        g = g + jnp.where(pos >= sh, pltpu.roll(g, sh, axis), 0.0)
        sh *= 2
    return g


def _proj_ln_kernel(alpha, a_ref, x_ref, w_ref, g_ref, b_ref, o_ref):
    sub = jnp.dot(a_ref[...].astype(BF16), w_ref[...], preferred_element_type=F32)
    o_ref[...] = _layer_norm(alpha * x_ref[...] + sub, g_ref[...], b_ref[...])


def _proj_ln(a, x, w, g, b, alpha, tm):
    m, d = x.shape
    e = a.shape[1]
    row = lambda i: (i, 0)
    fixed = lambda i: (0, 0)
    return pl.pallas_call(
        functools.partial(_proj_ln_kernel, alpha),
        grid=(m // tm,),
        in_specs=[pl.BlockSpec((tm, e), row), pl.BlockSpec((tm, d), row),
                  pl.BlockSpec((e, d), fixed), pl.BlockSpec((1, d), fixed),
                  pl.BlockSpec((1, d), fixed)],
        out_specs=pl.BlockSpec((tm, d), row),
        out_shape=jax.ShapeDtypeStruct((m, d), F32),
        compiler_params=_params("parallel"),
        name="proj_ln",
    )(a, x, w, g, b)


def _mlp_kernel(alpha, x_ref, wu_ref, wd_ref, g_ref, b_ref, o_ref, acc_ref):
    f = pl.program_id(1)
    tm = x_ref.shape[0]
    ngrp = 2 if tm % (2 * 2 * SUBLANES) == 0 else 1
    grp = [slice(r * tm // ngrp, (r + 1) * tm // ngrp) for r in range(ngrp)]

    @pl.when(f == 0)
    def _():
        acc_ref[...] = jnp.zeros_like(acc_ref)

    xb = [x_ref[g, :].astype(BF16) for g in grp]
    h = [jnp.dot(xb[r], wu_ref[...], preferred_element_type=F32) for r in range(ngrp)]
    h = [jnp.square(jnp.maximum(h[r], 0.0)).astype(BF16) for r in range(ngrp)]
    part = [jnp.dot(h[r], wd_ref[...], preferred_element_type=F32) for r in range(ngrp)]
    for r in range(ngrp):
        acc_ref[grp[r], :] += part[r]

    @pl.when(f == pl.num_programs(1) - 1)
    def _():
        o_ref[...] = _layer_norm(alpha * x_ref[...] + acc_ref[...], g_ref[...], b_ref[...])


def _mlp_ln(x, w_up, w_down, g, b, alpha, tm, tf):
    m, d = x.shape
    ff = w_up.shape[1]
    return pl.pallas_call(
        functools.partial(_mlp_kernel, alpha),
        grid=(m // tm, ff // tf),
        in_specs=[pl.BlockSpec((tm, d), lambda i, f: (i, 0)),
                  pl.BlockSpec((d, tf), lambda i, f: (0, f)),
                  pl.BlockSpec((tf, d), lambda i, f: (f, 0)),
                  pl.BlockSpec((1, d), lambda i, f: (0, 0)),
                  pl.BlockSpec((1, d), lambda i, f: (0, 0))],
        out_specs=pl.BlockSpec((tm, d), lambda i, f: (i, 0)),
        out_shape=jax.ShapeDtypeStruct((m, d), F32),
        scratch_shapes=[pltpu.VMEM((tm, d), F32)],
        compiler_params=_params("parallel", "arbitrary"),
        name="mlp_ln",
    )(x, w_up, w_down, g, b)


def _kvq_kernel(x_ref, w_ref, k_ref, v_ref, q_ref):
    e = k_ref.shape[1]
    xb = x_ref[...].astype(BF16)
    k_ref[...] = jnp.dot(xb, w_ref[:, 0:e], preferred_element_type=F32)
    v_ref[...] = jnp.dot(xb, w_ref[:, e:2 * e], preferred_element_type=F32)
    q_ref[...] = jnp.dot(xb, w_ref[:, 2 * e:3 * e], preferred_element_type=F32)


def _kvq(x, w_kvq, tm):
    m, d = x.shape
    e = w_kvq.shape[1] // 3
    row = lambda i: (i, 0)
    return pl.pallas_call(
        _kvq_kernel,
        grid=(m // tm,),
        in_specs=[pl.BlockSpec((tm, d), row), pl.BlockSpec((d, 3 * e), lambda i: (0, 0))],
        out_specs=[pl.BlockSpec((tm, e), row)] * 3,
        out_shape=[jax.ShapeDtypeStruct((m, e), F32)] * 3,
        compiler_params=_params("parallel"),
        name="kvq_proj",
    )(x, w_kvq)


def _kvq_prompt_kernel(x_ref, wk_ref, wvt_ref, wqt_ref, k_ref, kt_ref, vt_ref, qt_ref, km_ref):
    xb = x_ref[...].astype(BF16)
    k = jnp.dot(xb, wk_ref[...], preferred_element_type=F32)
    k_ref[...] = k
    kt_ref[0] = k.T
    vt_ref[0] = _bdot_nt(wvt_ref[...], xb)
    qt_ref[0] = _bdot_nt(wqt_ref[...], xb)
    km_ref[0] = jnp.sum(k, axis=0, keepdims=True) * (1.0 / k.shape[0])


def _kvq_prompt(x, w_k, w_vt, w_qt, b, l):
    m, d = x.shape
    e = w_k.shape[1]
    nb = l // MOBA_BLOCK
    fixed = lambda i, j: (0, 0)
    chan = pl.BlockSpec((1, e, MOBA_BLOCK), lambda i, j: (i, 0, j))
    chan_shape = jax.ShapeDtypeStruct((b, e, l), F32)
    return pl.pallas_call(
        _kvq_prompt_kernel,
        grid=(b, nb),
        in_specs=[pl.BlockSpec((MOBA_BLOCK, d), lambda i, j: (i * nb + j, 0)),
                  pl.BlockSpec((d, e), fixed), pl.BlockSpec((e, d), fixed),
                  pl.BlockSpec((e, d), fixed)],
        out_specs=[pl.BlockSpec((MOBA_BLOCK, e), lambda i, j: (i * nb + j, 0)), chan, chan, chan,
                   pl.BlockSpec((1, 1, e), lambda i, j: (i * nb + j, 0, 0))],
        out_shape=[jax.ShapeDtypeStruct((m, e), F32), chan_shape, chan_shape, chan_shape,
                   jax.ShapeDtypeStruct((m // MOBA_BLOCK, 1, e), F32)],
        compiler_params=_params("parallel", "parallel"),
        name="kvq_prompt",
    )(x, w_k, w_vt, w_qt)


def _gate_terms(ab, a_log, dt_bias):
    return -jnp.exp(a_log) * _softplus(ab + dt_bias)


def _gdn_in_kernel(x_ref, w_ref, wab_ref, wabt_ref, wc_ref, alog_r, dtb_r, alog_c, dtb_c,
                   q_ref, k_ref, v_ref, z_ref, gcol_ref, grow_ref, cst_ref,
                   ext_scr):
    t = pl.program_id(1)
    tm = x_ref.shape[1]
    hw = q_ref.shape[1]
    kw = wc_ref.shape[0]
    dk = hw // GDN_HEADS
    xb = x_ref[0].astype(BF16)

    @pl.when(t == 0)
    def _():
        ext_scr[:, 0:SUBLANES, :] = jnp.zeros((3, SUBLANES, hw), F32)

    outs = (q_ref, k_ref, v_ref)
    bw = 2 * dk
    for s in range(3):
        for c in range(hw // bw):
            cols = slice(s * hw + c * bw, s * hw + (c + 1) * bw)
            blk = slice(c * bw, (c + 1) * bw)
            pre = jnp.dot(xb, w_ref[:, cols], preferred_element_type=F32)
            ext_scr[s, SUBLANES:SUBLANES + tm, blk] = pre
            conv = pre * wc_ref[kw - 1:kw, cols]
            for sh in range(1, kw):
                conv = conv + (ext_scr[s, pl.ds(SUBLANES - sh, tm), blk]
                               * wc_ref[kw - 1 - sh:kw - sh, cols])
            act = _silu(conv)
            if s == 2:
                outs[s][:, blk] = act
            else:
                scale = dk ** -0.5 if s == 0 else 1.0
                for h in range(bw // dk):
                    sl = act[:, h * dk:(h + 1) * dk]
                    ss = jnp.sum(sl * sl, -1, keepdims=True)
                    outs[s][:, c * bw + h * dk:c * bw + (h + 1) * dk] = (
                        sl * (lax.rsqrt(ss + RMS_EPS) * scale))

    z_ref[...] = jnp.dot(xb, w_ref[:, 3 * hw:4 * hw], preferred_element_type=F32)

    ab = jnp.dot(xb, wab_ref[...], preferred_element_type=F32)
    lane = lax.broadcasted_iota(jnp.int32, ab.shape, 1)
    g = jnp.where(lane < GDN_HEADS, _gate_terms(ab, alog_r[...], dtb_r[...]), 0.0)
    gcol_ref[...] = jnp.where(lane < GDN_HEADS, _chunk_cumsum(g, 0), jax.nn.sigmoid(ab))

    abt = _bdot_nt(wabt_ref[...], xb)
    gct = _chunk_cumsum(_gate_terms(abt, alog_c[...], dtb_c[...]), 1)
    for c in range(tm // GDN_CHUNK):
        grow_ref[0, c] = gct[0:GDN_HEADS, c * GDN_CHUNK:(c + 1) * GDN_CHUNK]

    @pl.when(t == pl.num_programs(1) - 1)
    def _():
        for s in range(3):
            cst_ref[0, :, s * hw:(s + 1) * hw] = ext_scr[
                s, pl.ds(SUBLANES + tm - (kw - 1), kw - 1), :]

    for s in range(3):
        ext_scr[s, 0:SUBLANES, :] = ext_scr[s, tm:tm + SUBLANES, :]


def _gdn_in(x, w_qkvz, w_ab, w_abt, w_conv, alog_r, dtb_r, alog_c, dtb_c, tm):
    b, l, d = x.shape
    hw = w_qkvz.shape[1] // 4
    kw = w_conv.shape[0]
    nt = l // tm
    cpt = tm // GDN_CHUNK
    row = lambda i, t: (i * nt + t, 0)
    fixed = lambda i, t: (0, 0)
    act = jax.ShapeDtypeStruct((b * l, hw), F32)
    return pl.pallas_call(
        _gdn_in_kernel,
        grid=(b, nt),
        in_specs=[pl.BlockSpec((1, tm, d), lambda i, t: (i, t, 0)),
                  pl.BlockSpec(w_qkvz.shape, fixed), pl.BlockSpec(w_ab.shape, fixed),
                  pl.BlockSpec(w_abt.shape, fixed), pl.BlockSpec(w_conv.shape, fixed),
                  pl.BlockSpec(alog_r.shape, fixed), pl.BlockSpec(dtb_r.shape, fixed),
                  pl.BlockSpec(alog_c.shape, fixed), pl.BlockSpec(dtb_c.shape, fixed)],
        out_specs=[pl.BlockSpec((tm, hw), row)] * 4 + [
            pl.BlockSpec((tm, LANES), row),
            pl.BlockSpec((1, cpt, GDN_HEADS, GDN_CHUNK), lambda i, t: (i, t, 0, 0)),
            pl.BlockSpec((1, kw - 1, 3 * hw), lambda i, t: (i, 0, 0))],
        out_shape=[act] * 4 + [
            jax.ShapeDtypeStruct((b * l, LANES), F32),
            jax.ShapeDtypeStruct((b, l // GDN_CHUNK, GDN_HEADS, GDN_CHUNK), F32),
            jax.ShapeDtypeStruct((b, kw - 1, 3 * hw), F32)],
        scratch_shapes=[pltpu.VMEM((3, SUBLANES + tm, hw), F32)],
        compiler_params=_params("parallel", "arbitrary"),
        name="gdn_in",
    )(x, w_qkvz, w_ab, w_abt, w_conv, alog_r, dtb_r, alog_c, dtb_c)


def _gdn_chunk_kernel(alpha, q_ref, k_ref, v_ref, z_ref, gcol_ref, grow_ref, wn_ref, x_ref, wo_ref,
                      lng_ref, lnb_ref, y_ref, sfin_ref, s_scr, o_scr):
    c = pl.program_id(1)
    cs = GDN_CHUNK
    nseq, tr, hw = q_ref.shape[1:]
    dk = hw // GDN_HEADS
    cpb = tr // cs

    @pl.when(c == 0)
    def _():
        s_scr[...] = jnp.zeros_like(s_scr)

    ri = lax.broadcasted_iota(jnp.int32, (cs, cs), 0)
    ci = lax.broadcasted_iota(jnp.int32, (cs, cs), 1)
    incl = ri >= ci
    strict = ri > ci
    eye = (ri == ci).astype(F32)
    heads = range(GDN_HEADS)
    units = [(s, j, h) for s in range(nseq) for j in range(cpb) for h in heads]
    chains = [(s, h) for s in range(nseq) for h in heads]
    rows = {j: slice(j * cs, (j + 1) * cs) for j in range(cpb)}
    cols = {h: slice(h * dk, (h + 1) * dk) for h in heads}
    each = lambda f: {unit: f(*unit) for unit in units}
    gcol = {(s, j): gcol_ref[0, s, rows[j], :] for s in range(nseq) for j in range(cpb)}
    grow = {(s, j): grow_ref[0, s, j] for s in range(nseq) for j in range(cpb)}
    q = each(lambda s, j, h: q_ref[0, s, rows[j], cols[h]])
    k = each(lambda s, j, h: k_ref[0, s, rows[j], cols[h]])
    qb = each(lambda s, j, h: q[s, j, h].astype(BF16))
    kbf = each(lambda s, j, h: k[s, j, h].astype(BF16))
    gcc = each(lambda s, j, h: gcol[s, j][:, h:h + 1])
    beta = each(lambda s, j, h: gcol[s, j][:, GDN_HEADS + h:GDN_HEADS + h + 1])
    decay = each(lambda s, j, h: jnp.exp(
        jnp.where(incl, gcc[s, j, h] - grow[s, j][h:h + 1, :], NEG)))
    eg = each(lambda s, j, h: jnp.exp(gcc[s, j, h]))
    glast = each(lambda s, j, h: gcc[s, j, h][cs - 1:cs, :])
    kb = each(lambda s, j, h: k[s, j, h] * beta[s, j, h])
    kk = each(lambda s, j, h: _bdot_nt(kb[s, j, h], kbf[s, j, h]))
    qk = each(lambda s, j, h: _bdot_nt(qb[s, j, h], kbf[s, j, h]))
    p = each(lambda s, j, h: jnp.where(strict, -kk[s, j, h] * decay[s, j, h], 0.0))
    tinv = each(lambda s, j, h: eye + p[s, j, h])
    for _ in range(int(math.log2(cs)) - 1):
        pb = each(lambda s, j, h: p[s, j, h].astype(BF16))
        p = each(lambda s, j, h: _bdot(pb[s, j, h], pb[s, j, h]))
        tinv = each(lambda s, j, h: tinv[s, j, h] + _bdot(tinv[s, j, h], p[s, j, h]))
    tb = each(lambda s, j, h: tinv[s, j, h].astype(BF16))
    u = each(lambda s, j, h: _bdot(tb[s, j, h], v_ref[0, s, rows[j], cols[h]] * beta[s, j, h]))
    w = each(lambda s, j, h: _bdot(tb[s, j, h], kb[s, j, h] * eg[s, j, h]).astype(BF16))
    qg = each(lambda s, j, h: (q[s, j, h] * eg[s, j, h]).astype(BF16))
    attn = each(lambda s, j, h: (qk[s, j, h] * decay[s, j, h]).astype(BF16))
    ktail = each(lambda s, j, h: (k[s, j, h]
                                  * jnp.exp(glast[s, j, h] - gcc[s, j, h])).astype(BF16))
    st = {(s, h): s_scr[s, h] for s, h in chains}
    for j in range(cpb):
        sb = {(s, h): st[s, h].astype(BF16) for s, h in chains}
        v_new = {(s, h): (u[s, j, h] - _bdot(w[s, j, h], sb[s, h])).astype(BF16)
                 for s, h in chains}
        o = {(s, h): _bdot(qg[s, j, h], sb[s, h]) + _bdot(attn[s, j, h], v_new[s, h])
             for s, h in chains}
        st = {(s, h): (st[s, h] * jnp.exp(glast[s, j, h])
                       + _bdot_tn(ktail[s, j, h], v_new[s, h])) for s, h in chains}
        for s, h in chains:
            on = (o[s, h] * lax.rsqrt(jnp.mean(o[s, h] * o[s, h], -1, keepdims=True) + RMS_EPS)
                  * wn_ref[...])
            o_scr[s, rows[j], cols[h]] = (
                on * _silu(z_ref[0, s, rows[j], cols[h]])).astype(o_scr.dtype)
    for s, h in chains:
        s_scr[s, h] = st[s, h]
    for s in range(nseq):
        mix = jnp.dot(o_scr[s], wo_ref[...], preferred_element_type=F32)
        y_ref[0, s] = _layer_norm(alpha * x_ref[0, s] + mix, lng_ref[...], lnb_ref[...])

    @pl.when(c == pl.num_programs(1) - 1)
    def _():
        sfin_ref[0] = s_scr[...]


def _gdn_chunks(q, k, v, z, gcol, grow, w_onorm, x, w_o, ln_g, ln_b, alpha, b, l, nseq, cpb):
    hw = q.shape[1]
    d = x.shape[1]
    dk = hw // GDN_HEADS
    tr = cpb * GDN_CHUNK
    nc = l // tr
    assert b % nseq == 0 and l % tr == 0
    seqs = lambda a: a.reshape((b // nseq, nseq, l) + a.shape[1:])
    tile = lambda w_: pl.BlockSpec((1, nseq, tr, w_), lambda i, c: (i, 0, c, 0))
    fixed = lambda i, c: (0, 0)
    y, s_fin = pl.pallas_call(
        functools.partial(_gdn_chunk_kernel, alpha),
        grid=(b // nseq, nc),
        in_specs=[tile(hw)] * 4 + [
            tile(LANES),
            pl.BlockSpec((1, nseq, cpb, GDN_HEADS, GDN_CHUNK), lambda i, c: (i, 0, c, 0, 0)),
            pl.BlockSpec((1, dk), fixed), tile(d), pl.BlockSpec((hw, d), fixed),
            pl.BlockSpec((1, d), fixed), pl.BlockSpec((1, d), fixed)],
        out_specs=[tile(d),
                   pl.BlockSpec((1, nseq, GDN_HEADS, dk, dk), lambda i, c: (i, 0, 0, 0, 0))],
        out_shape=[jax.ShapeDtypeStruct((b // nseq, nseq, l, d), F32),
                   jax.ShapeDtypeStruct((b // nseq, nseq, GDN_HEADS, dk, dk), F32)],
        scratch_shapes=[pltpu.VMEM((nseq, GDN_HEADS, dk, dk), F32),
                        pltpu.VMEM((nseq, tr, hw), BF16)],
        compiler_params=_params("parallel", "arbitrary"),
        name="gdn_chunks",
    )(seqs(q), seqs(k), seqs(v), seqs(z), seqs(gcol),
      grow.reshape(b // nseq, nseq, l // GDN_CHUNK, GDN_HEADS, GDN_CHUNK), w_onorm, seqs(x), w_o,
      ln_g, ln_b)
    return y.reshape(b * l, d), s_fin.reshape(b, GDN_HEADS, dk, dk)


def _gdn_in_sample_kernel(x_ref, w_ref, wab_ref, wc_ref, alog_r, dtb_r, sc_ref,
                          q_ref, k_ref, v_ref, z_ref, gate_ref, cst_ref):
    hw = q_ref.shape[1]
    kw = wc_ref.shape[0]
    dk = hw // GDN_HEADS
    xb = x_ref[...].astype(BF16)
    outs = (q_ref, k_ref, v_ref)
    for s in range(3):
        cols = slice(s * hw, (s + 1) * hw)
        pre = jnp.dot(xb, w_ref[:, cols], preferred_element_type=F32)
        conv = pre * wc_ref[kw - 1:kw, cols]
        for j in range(kw - 1):
            conv = conv + sc_ref[j, :, cols] * wc_ref[j:j + 1, cols]
        cst_ref[kw - 2, :, cols] = pre
        act = _silu(conv)
        if s == 2:
            outs[s][...] = act
        else:
            scale = dk ** -0.5 if s == 0 else 1.0
            for h in range(GDN_HEADS):
                hs = slice(h * dk, (h + 1) * dk)
                sl = act[:, hs]
                ss = jnp.sum(sl * sl, -1, keepdims=True)
                outs[s][:, hs] = sl * (lax.rsqrt(ss + RMS_EPS) * scale)
    cst_ref[0:kw - 2] = sc_ref[1:kw - 1]
    z_ref[...] = jnp.dot(xb, w_ref[:, 3 * hw:4 * hw], preferred_element_type=F32)
    ab = jnp.dot(xb, wab_ref[...], preferred_element_type=F32)
    lane = lax.broadcasted_iota(jnp.int32, ab.shape, 1)
    gate_ref[...] = jnp.where(lane < GDN_HEADS, _gate_terms(ab, alog_r[...], dtb_r[...]),
                              jax.nn.sigmoid(ab))


def _gdn_in_sample(x, w_qkvz, w_ab, w_conv, alog_r, dtb_r, conv_state):
    m = x.shape[0]
    hw = w_qkvz.shape[1] // 4
    act = jax.ShapeDtypeStruct((m, hw), F32)
    return pl.pallas_call(
        _gdn_in_sample_kernel,
        out_shape=[act] * 4 + [jax.ShapeDtypeStruct((m, LANES), F32),
                               jax.ShapeDtypeStruct(conv_state.shape, F32)],
        compiler_params=pltpu.CompilerParams(vmem_limit_bytes=VMEM_LIMIT_BYTES),
        name="gdn_in_sample",
    )(x, w_qkvz, w_ab, w_conv, alog_r, dtb_r, conv_state)


def _gdn_step_kernel(q_ref, k_ref, v_ref, z_ref, gate_ref, wn_ref, s_ref, o_ref, snew_ref):
    dk = s_ref.shape[2]
    gate = gate_ref[0]
    heads = range(GDN_HEADS)
    hs = [slice(h * dk, (h + 1) * dk) for h in heads]
    q = [q_ref[0, :, hs[h]] for h in heads]
    k = [k_ref[0, :, hs[h]] for h in heads]
    v = [v_ref[0, :, hs[h]] for h in heads]
    eg = [jnp.exp(gate[:, h:h + 1]) for h in heads]
    beta = [gate[:, GDN_HEADS + h:GDN_HEADS + h + 1] for h in heads]
    s = [s_ref[0, h] for h in heads]
    kc = [jnp.broadcast_to(k[h], (dk, dk)).T for h in heads]
    qc = [jnp.broadcast_to(q[h], (dk, dk)).T for h in heads]
    ks = [jnp.sum(kc[h] * s[h], axis=0, keepdims=True) for h in heads]
    qs = [jnp.sum(qc[h] * s[h], axis=0, keepdims=True) for h in heads]
    v_new = [beta[h] * (v[h] - eg[h] * ks[h]) for h in heads]
    o = [eg[h] * qs[h] + jnp.sum(q[h] * k[h], -1, keepdims=True) * v_new[h] for h in heads]
    for h in heads:
        snew_ref[0, h] = eg[h] * s[h] + kc[h] * v_new[h]
    for h in heads:
        on = o[h] * lax.rsqrt(jnp.mean(o[h] * o[h], -1, keepdims=True) + RMS_EPS) * wn_ref[...]
        o_ref[0, :, hs[h]] = on * _silu(z_ref[0, :, hs[h]])


def _gdn_step(q, k, v, z, gate, w_onorm, state):
    m, hw = q.shape
    dk = hw // GDN_HEADS
    r3 = lambda a: a.reshape(m, 1, a.shape[1])
    tok = pl.BlockSpec((1, 1, hw), lambda i: (i, 0, 0))
    st = pl.BlockSpec((1, GDN_HEADS, dk, dk), lambda i: (i, 0, 0, 0))
    o, s_new = pl.pallas_call(
        _gdn_step_kernel,
        grid=(m,),
        in_specs=[tok] * 4 + [pl.BlockSpec((1, 1, LANES), lambda i: (i, 0, 0)),
                              pl.BlockSpec((1, dk), lambda i: (0, 0)), st],
        out_specs=[tok, st],
        out_shape=[jax.ShapeDtypeStruct((m, 1, hw), F32),
                   jax.ShapeDtypeStruct(state.shape, F32)],
        compiler_params=_params("parallel"),
        name="gdn_step",
    )(r3(q), r3(k), r3(v), r3(z), r3(gate), w_onorm, state)
    return o.reshape(m, hw), s_new


def _zigzag(step, n):
    return jnp.where(step % 2 == 0, step // 2, n - 1 - step // 2)


def _select_init(t, q_ref, qcol_scr, gate_scr):
    @pl.when(t == 0)
    def _():
        e, page = qcol_scr.shape
        qcol_scr[...] = jnp.broadcast_to(q_ref[0], (page, e)).T
        gate_scr[...] = jnp.zeros_like(gate_scr)


def _select_fold(ppb, t, pages, qcol_scr, gate_scr):
    e, page = qcol_scr.shape
    hd = e // MOBA_HEADS
    bps = len(pages) // ppb
    lane = lax.broadcasted_iota(jnp.int32, (MOBA_HEADS, LANES), 1)
    gate = gate_scr[...]
    qcol = qcol_scr[...]
    for blk in range(bps):
        tot = pages[blk * ppb][0]
        for j in range(1, ppb):
            tot = tot + pages[blk * ppb + j][0]
        per_head = jnp.sum((tot * qcol).reshape(MOBA_HEADS, hd, page), axis=1)
        g = jnp.sum(per_head, -1, keepdims=True) * (1.0 / MOBA_BLOCK)
        gate = jnp.where(lane == t * bps + blk, g, gate)
    gate_scr[...] = gate


def _select_pick(n_sel, nbf, last, idx_ref, gate_scr):
    @pl.when(last)
    def _():
        lane = lax.broadcasted_iota(jnp.int32, (MOBA_HEADS, LANES), 1)
        lane_f = lane.astype(F32)
        g = jnp.where(lane < nbf, gate_scr[...], -jnp.inf)
        picks = jnp.zeros((MOBA_HEADS, LANES), F32)
        for r in range(n_sel):
            best = jnp.max(g, -1, keepdims=True)
            idx = jnp.min(jnp.where(g == best, lane_f, float(LANES)), -1, keepdims=True)
            picks = jnp.where(lane == r, idx, picks)
            g = jnp.where(lane_f == idx, -jnp.inf, g)
        idx_ref[0] = picks.astype(jnp.int32)


def _moba_prompt_kernel(n_sel, n_sel_s, ppb, nbf, pt_ref, slopes_ref, qt_lo_ref, qt_hi_ref, k_ref,
                        vt_ref, km_ref, qs_ref, *rest):
    del pt_ref
    pages = rest[:-4]
    o_ref, idx_ref, qcol_scr, gate_scr = rest[-4:]
    p = pl.program_id(1)
    half = pl.num_programs(2)
    a = _zigzag(pl.program_id(2), half)
    step = (pl.program_id(0) * pl.num_programs(1) + p) * pl.num_programs(2) + pl.program_id(2)
    sps = nbf * ppb // len(pages)
    t = step % sps
    _select_init(t, qs_ref, qcol_scr, gate_scr)
    bs = MOBA_BLOCK
    hd = LANES // 2
    nb = km_ref.shape[1]
    nbp = -(-nb // SUBLANES) * SUBLANES
    rel = (lax.broadcasted_iota(jnp.int32, (bs, bs), 1)
           - lax.broadcasted_iota(jnp.int32, (bs, bs), 0))
    aug_lane = lax.broadcasted_iota(jnp.int32, (bs, LANES), 1)
    k_idx = lax.broadcasted_iota(jnp.int32, (bs, LANES), 0).astype(F32)
    k_aug = jnp.where(aug_lane < 3, k_idx, jnp.where(aug_lane < 6, 1.0, 0.0)).astype(BF16)

    def query_side(qt_ref, i):
        qt = qt_ref[0]
        km = km_ref[0]
        if nbp > nb:
            km = jnp.concatenate([km, jnp.zeros((nbp - nb, LANES), F32)], axis=0)
        k_hi, k_mid, _ = _split3(km)
        chan = lax.broadcasted_iota(jnp.int32, (LANES, bs), 0)
        blk = lax.broadcasted_iota(jnp.int32, (nbp, bs), 0)
        q_idx = lax.broadcasted_iota(jnp.int32, (1, bs), 1).astype(F32)
        aug_row = lax.broadcasted_iota(jnp.int32, (2 * SUBLANES, bs), 0)
        wqs, keeps, slope_h = [], [], []
        for hh in range(2):
            qh = jnp.where((chan // hd) == hh, qt, 0.0)
            q_hi, q_mid, _ = _split3(qh)
            gate = _bdot(k_hi, q_hi) + _bdot(k_hi, q_mid) + _bdot(k_mid, q_hi)
            cnt = jnp.zeros((nbp, bs), jnp.int32)
            for m in range(i):
                gm = gate[m:m + 1, :]
                beats = (gm > gate) | ((gm == gate) & (m < blk))
                cnt = cnt + jnp.where(beats, 1, 0)
            keeps.append((blk < i) & (cnt < n_sel))
            slope2 = slopes_ref[2 * p + hh] * LOG2E
            slope_h.append(slope2)
            s_parts = _split3(jnp.full((1, bs), slope2, F32))
            t_parts = _split3(-slope2 * q_idx)
            aug = jnp.zeros((2 * SUBLANES, bs), F32)
            for r, part in enumerate(s_parts + t_parts):
                aug = jnp.where(aug_row == r, part.astype(F32), aug)
            wqs.append(jnp.concatenate(
                [(qh * (hd ** -0.5 * LOG2E)).astype(BF16), aug.astype(BF16),
                 jnp.zeros((LANES - 2 * SUBLANES, bs), BF16)], axis=0))
        return wqs, keeps, slope_h

    def attend(n, qt_ref, slot):
        wqs, keeps, slope_h = query_side(qt_ref, n - 1)
        pair = range(2)
        kcat = jnp.concatenate([k_ref[0:n * bs, :].astype(BF16),
                                jnp.concatenate([k_aug] * n, axis=0)], axis=1)
        s_all = [jnp.dot(kcat, wqs[hh], preferred_element_type=F32) for hh in pair]
        us, shifts = [], []
        for hh in pair:
            u = [s_all[hh][j * bs:(j + 1) * bs, :] for j in range(n - 1)]
            u.append(jnp.where(rel >= 0, s_all[hh][(n - 1) * bs:n * bs, :], NEG))
            rows = [jnp.where(keeps[hh][j:j + 1, :], -slope_h[hh] * float((n - 1 - j) * bs), NEG)
                    for j in range(n - 1)] + [jnp.zeros((1, bs), F32)]
            mx = functools.reduce(
                jnp.maximum, [jnp.max(u[j], 0, keepdims=True) + rows[j] for j in range(n)])
            us.append(u)
            shifts.append([mx - rows[j] for j in range(n)])
        ps = [[jnp.exp2(us[hh][j] - shifts[hh][j]) for j in range(n)] for hh in pair]
        den = [sum(jnp.sum(pm, 0, keepdims=True) for pm in ps[hh]) for hh in pair]
        pcat = [jnp.concatenate([pm.astype(BF16) for pm in ps[hh]], axis=0) for hh in pair]
        acc = [_bdot(vt_ref[0, hh * hd:(hh + 1) * hd, 0:n * bs], pcat[hh]) for hh in pair]
        o_ref[0, slot, 0] = jnp.concatenate([acc[hh] / den[hh] for hh in pair],
                                            axis=0).T.astype(o_ref.dtype)

    def attend_two(a_static):
        _select_fold(ppb, t, pages, qcol_scr, gate_scr)
        attend(a_static + 1, qt_lo_ref, 0)
        attend(a_static + 1 + nb // 2, qt_hi_ref, 1)

    for a_static in range(nb // 2):
        pl.when(a == a_static)(functools.partial(attend_two, a_static))
    _select_pick(n_sel_s, nbf, t == sps - 1, idx_ref, gate_scr)


def _moba_prompt(qt, k, vt, k_mean, slopes, b, l, q_s, cache_kt, page_table, n_sel_s):
    e = k.shape[1]
    nb = l // MOBA_BLOCK
    n_sel = min(MOBA_TOPK, (l - 1) // MOBA_BLOCK)
    npair = e // LANES
    m = q_s.shape[0]
    _, _, page = cache_kt.shape
    n_pages = page_table.shape[1]
    ppb = MOBA_BLOCK // page
    nbf = n_pages // ppb
    assert nb % 2 == 0
    half = nb // 2
    steps = b * npair * half
    assert steps % m == 0 and nbf <= LANES and page == LANES
    sps = steps // m
    assert n_pages % sps == 0 and (n_pages // sps) % ppb == 0
    pps = n_pages // sps
    seq = lambda bi, p, i: ((bi * npair + p) * half + i) // sps
    tstep = lambda bi, p, i: ((bi * npair + p) * half + i) % sps

    def page_spec(j):
        return pl.BlockSpec(
            (1, e, page), lambda bi, p, i, pt: (pt[seq(bi, p, i), tstep(bi, p, i) * pps + j], 0, 0))

    def qt_spec(upper):
        return pl.BlockSpec((1, LANES, MOBA_BLOCK),
                            lambda bi, p, i, pt: (bi, p, _zigzag(i, half) + upper * half))

    o, idx = pl.pallas_call(
        functools.partial(_moba_prompt_kernel, n_sel, n_sel_s, ppb, nbf),
        grid_spec=pltpu.PrefetchScalarGridSpec(
            num_scalar_prefetch=1,
            grid=(b, npair, half),
            in_specs=[pl.BlockSpec(memory_space=pltpu.SMEM), qt_spec(0), qt_spec(1),
                      pl.BlockSpec((l, LANES), lambda bi, p, i, pt: (bi, p)),
                      pl.BlockSpec((1, LANES, l), lambda bi, p, i, pt: (bi, p, 0)),
                      pl.BlockSpec((1, nb, LANES), lambda bi, p, i, pt: (bi, 0, p)),
                      pl.BlockSpec((1, 1, e), lambda bi, p, i, pt: (seq(bi, p, i), 0, 0))]
            + [page_spec(j) for j in range(pps)],
            out_specs=[pl.BlockSpec((1, 2, 1, MOBA_BLOCK, LANES),
                                    lambda bi, p, i, pt: (bi, 0, _zigzag(i, half), 0, p)),
                       pl.BlockSpec((1, MOBA_HEADS, LANES),
                                    lambda bi, p, i, pt: (seq(bi, p, i), 0, 0))],
            scratch_shapes=[pltpu.VMEM((e, page), F32), pltpu.VMEM((MOBA_HEADS, LANES), F32)]),
        out_shape=[jax.ShapeDtypeStruct((b, 2, half, MOBA_BLOCK, e), BF16),
                   jax.ShapeDtypeStruct((m, MOBA_HEADS, LANES), jnp.int32)],
        compiler_params=_params("arbitrary", "arbitrary", "arbitrary"),
        name="moba_prompt",
    )(page_table, slopes, qt, qt, k, vt, k_mean, q_s.reshape(m, 1, e), *([cache_kt] * pps))
    return o.reshape(b * l, e), idx


def _moba_sample_kernel(n_sel, ppb, past, hps, pt_ref, sel_ref, slopes_ref, q_ref, kn_ref, vn_ref,
                        *refs):
    del pt_ref
    tph = n_sel * ppb
    k_tiles = refs[:hps * tph]
    v_tiles = refs[hps * tph:2 * hps * tph]
    o_ref = refs[2 * hps * tph]
    bi = pl.program_id(0)
    grp = pl.program_id(1)
    hd, page = k_tiles[0].shape[1:]
    pos_in_page = lax.broadcasted_iota(jnp.int32, (SUBLANES, page), 1)
    q = q_ref[0]
    kn = kn_ref[0]
    vn = vn_ref[0]
    heads = range(hps)
    tiles = range(tph)
    ls = [slice(hh * hd, (hh + 1) * hd) for hh in heads]
    qh = [q[:, ls[hh]] * hd ** -0.5 for hh in heads]
    q8 = [jnp.broadcast_to(qh[hh], (SUBLANES, hd)).astype(BF16) for hh in heads]
    s_own = [jnp.sum(qh[hh] * kn[:, ls[hh]], -1, keepdims=True) for hh in heads]
    raw = [[jnp.dot(q8[hh], k_tiles[hh * tph + n][0].astype(BF16), preferred_element_type=F32)
            for n in tiles] for hh in heads]
    scores = []
    for hh in heads:
        h = grp * hps + hh
        slope = slopes_ref[h]
        row = []
        for n in tiles:
            blk = sel_ref[bi, h * n_sel + n // ppb]
            dist = (past - blk * MOBA_BLOCK - (n % ppb) * page - pos_in_page).astype(F32)
            row.append(raw[hh][n] - slope * dist)
        scores.append(row)
    mx = [functools.reduce(jnp.maximum,
                           [jnp.max(sc, -1, keepdims=True)[0:1] for sc in scores[hh]], s_own[hh])
          for hh in heads]
    pm = [[jnp.exp(sc - mx[hh]) for sc in scores[hh]] for hh in heads]
    pv = [[_bdot_nt(pm[hh][n], v_tiles[hh * tph + n][0])[0:1] for n in tiles] for hh in heads]
    outs = []
    for hh in heads:
        p_own = jnp.exp(s_own[hh] - mx[hh])
        den = p_own + sum(jnp.sum(pm[hh][n], -1, keepdims=True)[0:1] for n in tiles)
        acc = p_own * vn[:, ls[hh]] + sum(pv[hh])
        outs.append(acc / den)
    o_ref[0] = jnp.concatenate(outs, axis=-1)


def _moba_sample(q, k_new, v_new, cache_kt, cache_vt, page_table, sel, slopes, n_sel):
    m, e = q.shape
    _, _, page = cache_kt.shape
    hd = e // MOBA_HEADS
    ppb = MOBA_BLOCK // page
    past = page_table.shape[1] * page
    hps = MOBA_HEADS
    assert MOBA_HEADS % hps == 0 and (hps * hd) % LANES == 0
    r3 = lambda a: a.reshape(m, 1, e)
    tok = pl.BlockSpec((1, 1, hps * hd), lambda bi, g, pt, sl: (bi, 0, g))

    def tile_spec(hh, s, j):
        def index(bi, g, pt, sl):
            h = g * hps + hh
            return pt[bi, sl[bi, h * n_sel + s] * ppb + j], h, 0
        return pl.BlockSpec((1, hd, page), index)

    tiles = [tile_spec(hh, s, j) for hh in range(hps) for s in range(n_sel) for j in range(ppb)]
    out = pl.pallas_call(
        functools.partial(_moba_sample_kernel, n_sel, ppb, past, hps),
        grid_spec=pltpu.PrefetchScalarGridSpec(
            num_scalar_prefetch=2,
            grid=(m, MOBA_HEADS // hps),
            in_specs=[pl.BlockSpec(memory_space=pltpu.SMEM), tok, tok, tok] + tiles + tiles,
            out_specs=tok),
        out_shape=jax.ShapeDtypeStruct((m, 1, e), F32),
        compiler_params=_params("parallel", "parallel"),
        name="moba_sample",
    )(page_table, sel, slopes, r3(q), r3(k_new), r3(v_new),
      *([cache_kt] * len(tiles)), *([cache_vt] * len(tiles)))
    return out.reshape(m, e)


def _row_tile(m, cap):
    t = min(m, cap)
    assert m % t == 0
    return t


def kernel(x_prompt, x_sample, cache_k, cache_v, state_delta, state_conv, page_table,
           gdn_w_in, gdn_w_conv, gdn_a_log, gdn_dt_bias, gdn_w_onorm, gdn_w_o,
           w_kv, moba_w_q, moba_w_o, mlp_w_up, mlp_w_down, ln_g, ln_b):
    bp, lp, d = x_prompt.shape
    bs, ls, _ = x_sample.shape
    depth = mlp_w_up.shape[0]
    n_a = gdn_w_in.shape[0]
    alpha = (2 * depth) ** 0.25
    n_pool, page = cache_k.shape[0], cache_k.shape[1]
    e_kv = cache_k.shape[2] * cache_k.shape[3]
    hw = gdn_w_o.shape[1]
    conv_ch = gdn_w_conv.shape[2]
    kw = gdn_w_conv.shape[1]
    n_pages = page_table.shape[1]
    assert ls == 1, "sample stream handles one new token per sequence"
    assert conv_ch == 3 * hw and kw - 1 <= SUBLANES and kw >= 3
    assert lp % MOBA_BLOCK == 0 and MOBA_BLOCK % GDN_CHUNK == 0 and MOBA_BLOCK % page == 0
    assert (n_pages * page) % MOBA_BLOCK == 0, "past length must end on a MoBA block boundary"
    assert e_kv == MOBA_HEADS * (LANES // 2)
    assert depth - n_a == 1, "one MoBA layer reads the shared K/V"
    nbf = n_pages * page // MOBA_BLOCK
    n_sel_s = min(MOBA_TOPK, nbf)
    assert n_sel_s >= 1

    xp = x_prompt.reshape(bp * lp, d)
    xs = x_sample.reshape(bs, d)
    tm_p = _row_tile(bp * lp, 512)
    tm_s = _row_tile(bs, 512)
    row = lambda a: a.reshape(1, -1).astype(F32)
    heads = jnp.arange(1, MOBA_HEADS + 1, dtype=F32)
    slopes = jnp.exp2(-ALIBI_MAX_EXP * heads / MOBA_HEADS)
    cache_kt = cache_k.transpose(0, 2, 3, 1).reshape(n_pool, e_kv, page)
    cache_vt = cache_v.transpose(0, 2, 3, 1).reshape(n_pool, e_kv, page)

    conv_p, delta_p, conv_s, delta_s = [], [], [], []
    for layer in range(depth):
        g0, b0 = row(ln_g[layer, 0]), row(ln_b[layer, 0])
        g1, b1 = row(ln_g[layer, 1]), row(ln_b[layer, 1])
        if layer < n_a:
            w_in = gdn_w_in[layer]
            w_qkvz = w_in[:, :4 * hw].astype(BF16)
            w_gate = w_in[:, 4 * hw:]
            w_ab = jnp.pad(w_gate, ((0, 0), (0, LANES - 2 * GDN_HEADS))).astype(BF16)
            w_abt = w_gate.T.astype(BF16)
            pad_r = lambda a: jnp.pad(row(a), ((0, 0), (0, LANES - GDN_HEADS)))
            pad_c = lambda a: jnp.pad(a.astype(F32).reshape(-1, 1), ((0, GDN_HEADS), (0, 0)))
            alog_r, dtb_r = pad_r(gdn_a_log[layer]), pad_r(gdn_dt_bias[layer])
            alog_c, dtb_c = pad_c(gdn_a_log[layer]), pad_c(gdn_dt_bias[layer])
            w_conv = gdn_w_conv[layer].astype(F32)
            w_onorm = row(gdn_w_onorm[layer])
            w_o = gdn_w_o[layer].astype(BF16)

            q, k, v, z, gcol, grow, cst = _gdn_in(
                x_prompt if layer == 0 else xp.reshape(bp, lp, d),
                w_qkvz, w_ab, w_abt, w_conv, alog_r, dtb_r, alog_c, dtb_c, tm=MOBA_BLOCK)
            xp, s_fin = _gdn_chunks(q, k, v, z, gcol, grow, w_onorm, xp, w_o, g0, b0, alpha,
                                    bp, lp, nseq=2, cpb=4)
            conv_p.append(cst)
            delta_p.append(s_fin)
            mix_p = None

            q, k, v, z, gate, cst = _gdn_in_sample(
                xs, w_qkvz, w_ab, w_conv, alog_r, dtb_r,
                state_conv[layer].astype(F32).transpose(1, 0, 2))
            o, s_new = _gdn_step(q, k, v, z, gate, w_onorm, state_delta[layer].astype(F32))
            conv_s.append(cst.transpose(1, 0, 2))
            delta_s.append(s_new)
            mix_s = (o, w_o)
        else:
            w_q = moba_w_q[layer - n_a]
            k_row, kt_p, vt_p, qt_p, km_p = _kvq_prompt(
                xp, w_kv[:, :e_kv].astype(BF16), w_kv[:, e_kv:].T.astype(BF16),
                w_q.T.astype(BF16), bp, lp)
            k_s, v_s, q_s = _kvq(xs, jnp.concatenate([w_kv, w_q], axis=1).astype(BF16), tm_s)
            w_o = moba_w_o[layer - n_a].astype(BF16)
            o, sel = _moba_prompt(qt_p, k_row, vt_p, km_p.reshape(bp, lp // MOBA_BLOCK, e_kv),
                                  slopes, bp, lp, q_s, cache_kt, page_table, n_sel_s)
            mix_p = (o, w_o)
            sel = sel[:, :, :n_sel_s].reshape(bs, MOBA_HEADS * n_sel_s)
            o = _moba_sample(q_s, k_s, v_s, cache_kt, cache_vt, page_table, sel, slopes, n_sel_s)
            mix_s = (o, w_o)
        w_up = mlp_w_up[layer].astype(BF16)
        w_down = mlp_w_down[layer].astype(BF16)
        if mix_p is not None:
            xp = _proj_ln(mix_p[0], xp, mix_p[1], g0, b0, alpha, tm_p)
        xs = _proj_ln(mix_s[0], xs, mix_s[1], g0, b0, alpha, tm_s)
        xp = _mlp_ln(xp, w_up, w_down, g1, b1, alpha, tm_p, 2048)
        xs = _mlp_ln(xs, w_up, w_down, g1, b1, alpha, tm_s, 1024)

    hd = e_kv // MOBA_HEADS
    tokens_major = lambda a: a.reshape(bp, MOBA_HEADS, hd, lp).transpose(0, 3, 1, 2)
    return (xp.reshape(bp, lp, d), xs.reshape(bs, ls, d),
            tokens_major(kt_p), tokens_major(vt_p),
            jnp.stack(delta_p), jnp.stack(conv_p),
            k_s.reshape(bs, ls, MOBA_HEADS, hd), v_s.reshape(bs, ls, MOBA_HEADS, hd),
            jnp.stack(delta_s), jnp.stack(conv_s))
```
